```python
import jax, jax.numpy as jnp
from jax import lax
import numpy as np

D_MODEL = 1024
BATCH = 32
SEQ = 2048
DEPTH = 1

HEAD_DIM = 64
NSA_HEADS = 8
NSA_KV = 2
DSA_HEADS = 8
IDX_HEADS = 4
IDX_DIM = 64
CMP_LEN = 32
CMP_STRIDE = 16
CMP_HIDDEN = 2 * HEAD_DIM
SLC_LEN = 64
SLC_TOPN = 16
WIN = 512
DSA_TOPK = 256
D_FF = 4 * D_MODEL
D_MIX = (NSA_HEADS + DSA_HEADS) * HEAD_DIM
ROPE_THETA = 10000.0
EPS = 1e-6
Q_BLOCK = 128
SLC_Q_BLOCK = 16
NEG = -1e30
NSA_COLS = [NSA_HEADS * HEAD_DIM] + [NSA_KV * HEAD_DIM] * 6 + [NSA_HEADS * 3]
DSA_COLS = [DSA_HEADS * HEAD_DIM, HEAD_DIM, HEAD_DIM, IDX_HEADS * IDX_DIM, IDX_DIM, IDX_HEADS]
IN_COLS = NSA_COLS + DSA_COLS
D_IN = sum(IN_COLS)

kernel_name = 'hybrid_nsa_dsa_parallel_block'


def rms_norm(x, g):
    xf = x.astype(jnp.float32)
    y = xf * lax.rsqrt(jnp.mean(xf * xf, axis=-1, keepdims=True) + EPS)
    return (y * g.astype(jnp.float32)).astype(x.dtype)


def rope(x, pos):
    half = x.shape[-1] // 2
    inv = ROPE_THETA ** (-jnp.arange(half, dtype=jnp.float32) / half)
    ang = pos.astype(jnp.float32)[:, :, None] * inv
    cos = jnp.cos(ang)[:, :, None, :]
    sin = jnp.sin(ang)[:, :, None, :]
    xf = x.astype(jnp.float32)
    x1, x2 = xf[..., :half], xf[..., half:]
    return jnp.concatenate([x1 * cos - x2 * sin, x2 * cos + x1 * sin], axis=-1).astype(x.dtype)


def masked_softmax(s, mask):
    s = jnp.where(mask, s.astype(jnp.float32), NEG)
    return jax.nn.softmax(s, axis=-1) * mask.astype(jnp.float32)


def compress(kv, pe, w1, w2):
    B, T, G, dh = kv.shape
    n_cmp = (T - CMP_LEN) // CMP_STRIDE + 1
    blk = np.arange(n_cmp)[:, None] * CMP_STRIDE + np.arange(CMP_LEN)[None, :]
    blocks = kv[:, blk] + pe[None, None, :, None, :]
    flat = blocks.transpose(0, 1, 3, 2, 4).reshape(B, n_cmp, G, CMP_LEN * dh)
    return jax.nn.gelu(flat @ w1) @ w2


def nsa_attention(q, kc, vc, ks, vs, kw, vw, gate_logits, pe_k, pe_v, w1k, w2k, w1v, w2v):
    B, T, H, dh = q.shape
    G = kc.shape[2]
    R = H // G
    scale = dh ** -0.5
    q_t = q.reshape(B, T, G, R, dh).transpose(0, 2, 3, 1, 4)
    t_idx = np.arange(T)

    k_cmp = compress(kc, pe_k, w1k, w2k)
    v_cmp = compress(vc, pe_v, w1v, w2v)
    n_cmp = k_cmp.shape[1]
    cmp_start = np.arange(n_cmp) * CMP_STRIDE
    cmp_end = cmp_start + CMP_LEN - 1
    mask_c = jnp.asarray(cmp_end[None, :] <= t_idx[:, None])
    s_c = jnp.einsum('bgrtd,bngd->bgrtn', q_t, k_cmp) * scale
    p_c = masked_softmax(s_c, mask_c)
    o_cmp = jnp.einsum('bgrtn,bngd->bgrtd', p_c.astype(v_cmp.dtype), v_cmp)

    n_slc = T // SLC_LEN
    n_sel = min(SLC_TOPN, n_slc)
    slc_start = np.arange(n_slc) * SLC_LEN
    slc_end = slc_start + SLC_LEN - 1
    overlap = ((cmp_start[:, None] <= slc_end[None, :]) & (cmp_end[:, None] >= slc_start[None, :])).astype(np.float32)
    imp = jnp.einsum('bgrtn,nj->bgtj', p_c, jnp.asarray(overlap))
    j = np.arange(n_slc)[None, :]
    cur = (t_idx // SLC_LEN)[:, None]
    valid = jnp.asarray(j <= cur)
    forced = jnp.asarray((j == 0) | (j == cur) | (j == cur - 1))
    sel_score = jnp.where(valid, jnp.where(forced, jnp.inf, imp), -jnp.inf)
    _, sel = lax.top_k(sel_score, n_sel)

    kb = ks.reshape(B, n_slc, SLC_LEN, G, dh).transpose(0, 3, 1, 2, 4)
    vb = vs.reshape(B, n_slc, SLC_LEN, G, dh).transpose(0, 3, 1, 2, 4)
    nq = T // SLC_Q_BLOCK
    n_key = n_sel * SLC_LEN
    q_ch = q_t.reshape(B, G, R, nq, SLC_Q_BLOCK, dh).transpose(3, 0, 1, 2, 4, 5)
    i_ch = sel.reshape(B, G, nq, SLC_Q_BLOCK, n_sel).transpose(2, 0, 1, 3, 4)
    t_ch = jnp.arange(T, dtype=jnp.int32).reshape(nq, SLC_Q_BLOCK)
    bi = jnp.arange(B)[:, None, None, None]
    gi = jnp.arange(G)[None, :, None, None]

    def sel_chunk(args):
        qc, ic, tc = args
        kg = kb[bi, gi, ic].reshape(B, G, SLC_Q_BLOCK, n_key, dh)
        vg = vb[bi, gi, ic].reshape(B, G, SLC_Q_BLOCK, n_key, dh)
        kpos = (ic[..., None] * SLC_LEN + jnp.arange(SLC_LEN)).reshape(B, G, SLC_Q_BLOCK, n_key)
        mask = (kpos <= tc[None, None, :, None])[:, :, None]
        s = jnp.einsum('bgrqd,bgqkd->bgrqk', qc, kg) * scale
        p = masked_softmax(s, mask)
        return jnp.einsum('bgrqk,bgqkd->bgrqd', p.astype(vg.dtype), vg)

    o_slc = lax.map(sel_chunk, (q_ch, i_ch, t_ch))
    o_slc = o_slc.transpose(1, 2, 3, 0, 4, 5).reshape(B, G, R, T, dh)

    kp = jnp.pad(kw, ((0, 0), (WIN, 0), (0, 0), (0, 0)))
    vp = jnp.pad(vw, ((0, 0), (WIN, 0), (0, 0), (0, 0)))
    nb = T // Q_BLOCK
    span = WIN + Q_BLOCK
    q_b = q_t.reshape(B, G, R, nb, Q_BLOCK, dh).transpose(3, 0, 1, 2, 4, 5)

    def win_block(args):
        qb, bidx = args
        s0 = bidx * Q_BLOCK
        kk = lax.dynamic_slice_in_dim(kp, s0, span, axis=1)
        vv = lax.dynamic_slice_in_dim(vp, s0, span, axis=1)
        kpos = s0 - WIN + jnp.arange(span)
        tq = s0 + jnp.arange(Q_BLOCK)
        mask = (kpos[None, :] <= tq[:, None]) & (kpos[None, :] > tq[:, None] - WIN) & (kpos[None, :] >= 0)
        s = jnp.einsum('bgrqd,bsgd->bgrqs', qb, kk) * scale
        p = masked_softmax(s, mask)
        return jnp.einsum('bgrqs,bsgd->bgrqd', p.astype(vv.dtype), vv)

    o_win = lax.map(win_block, (q_b, jnp.arange(nb)))
    o_win = o_win.transpose(1, 2, 3, 0, 4, 5).reshape(B, G, R, T, dh)

    g = jax.nn.sigmoid(gate_logits.astype(jnp.float32)).reshape(B, T, G, R, 3)
    g = g.transpose(0, 2, 3, 1, 4).astype(q.dtype)
    o = g[..., 0:1] * o_cmp + g[..., 1:2] * o_slc + g[..., 2:3] * o_win
    return o.transpose(0, 3, 1, 2, 4).reshape(B, T, H * dh)


def dsa_attention(q, k, v, q_idx, k_idx, w_idx):
    B, T, H, dh = q.shape
    k_top = min(DSA_TOPK, T // 4)
    scale = dh ** -0.5
    idx_scale = (IDX_HEADS ** -0.5) * (IDX_DIM ** -0.5)
    nb = T // Q_BLOCK
    bi = jnp.arange(B)[:, None, None]
    key_pos = jnp.arange(T)

    def blocks(a):
        return a.reshape((B, nb, Q_BLOCK) + a.shape[2:]).swapaxes(0, 1)

    def dsa_block(args):
        qb, qib, wb, bidx = args
        tq = bidx * Q_BLOCK + jnp.arange(Q_BLOCK)
        logits = jnp.einsum('bqhd,bsd->bqhs', qib, k_idx).astype(jnp.float32)
        score = jnp.einsum('bqh,bqhs->bqs', wb.astype(jnp.float32) * idx_scale, jax.nn.relu(logits))
        score = jnp.where(key_pos[None, None, :] <= tq[None, :, None], score, -jnp.inf)
        _, sel = lax.top_k(score, k_top)
        kg = k[bi, sel]
        vg = v[bi, sel]
        mask = (sel <= tq[None, :, None])[:, None]
        s = jnp.einsum('bqhd,bqkd->bhqk', qb, kg) * scale
        p = masked_softmax(s, mask)
        return jnp.einsum('bhqk,bqkd->bqhd', p.astype(vg.dtype), vg)

    o = lax.map(dsa_block, (blocks(q), blocks(q_idx), blocks(w_idx), jnp.arange(nb)))
    return o.swapaxes(0, 1).reshape(B, T, H * dh)


def setup_inputs(seed: int = 0) -> dict:
    key = jax.random.key(seed)
    ks = jax.random.split(key, 20)

    def nrm(k, shape, s):
        return jax.random.normal(k, shape, jnp.float32) * s

    x = nrm(ks[0], (BATCH, SEQ, D_MODEL), 1.0)
    c = nrm(ks[1], (BATCH, D_MODEL), 1.0)
    start = jax.random.randint(ks[2], (BATCH, 1), 0, 4096, dtype=jnp.int32)
    positions = start + jnp.arange(SEQ, dtype=jnp.int32)[None, :]
    return {
        'x': x,
        'c': c,
        'positions': positions,
        'w_ada': nrm(ks[3], (DEPTH, D_MODEL, 6 * D_MODEL), 0.1 * D_MODEL ** -0.5),
        'b_ada': nrm(ks[4], (DEPTH, 6 * D_MODEL), 0.01),
        'g_pre_mix': 1.0 + nrm(ks[5], (DEPTH, D_MODEL), 0.05),
        'g_post_mix': 1.0 + nrm(ks[6], (DEPTH, D_MODEL), 0.05),
        'g_pre_ffn': 1.0 + nrm(ks[7], (DEPTH, D_MODEL), 0.05),
        'g_post_ffn': 1.0 + nrm(ks[8], (DEPTH, D_MODEL), 0.05),
        'w_in': nrm(ks[9], (DEPTH, D_MODEL, D_IN), D_MODEL ** -0.5),
        'cmp_pe_k': nrm(ks[10], (DEPTH, CMP_LEN, HEAD_DIM), 0.02),
        'cmp_pe_v': nrm(ks[11], (DEPTH, CMP_LEN, HEAD_DIM), 0.02),
        'cmp_w1_k': nrm(ks[12], (DEPTH, CMP_LEN * HEAD_DIM, CMP_HIDDEN), (CMP_LEN * HEAD_DIM) ** -0.5),
        'cmp_w2_k': nrm(ks[13], (DEPTH, CMP_HIDDEN, HEAD_DIM), CMP_HIDDEN ** -0.5),
        'cmp_w1_v': nrm(ks[14], (DEPTH, CMP_LEN * HEAD_DIM, CMP_HIDDEN), (CMP_LEN * HEAD_DIM) ** -0.5),
        'cmp_w2_v': nrm(ks[15], (DEPTH, CMP_HIDDEN, HEAD_DIM), CMP_HIDDEN ** -0.5),
        'w_out': nrm(ks[16], (DEPTH, D_MIX, D_MODEL), D_MIX ** -0.5),
        'w_up': nrm(ks[17], (DEPTH, D_MODEL, D_FF), D_MODEL ** -0.5),
        'w_down': nrm(ks[18], (DEPTH, D_FF, D_MODEL), D_FF ** -0.5),
    }


def reference(x, c, positions, w_ada, b_ada, g_pre_mix, g_post_mix, g_pre_ffn, g_post_ffn,
              w_in, cmp_pe_k, cmp_pe_v, cmp_w1_k, cmp_w2_k, cmp_w1_v, cmp_w2_v,
              w_out, w_up, w_down):
    B, T, _ = x.shape
    splits = np.cumsum(IN_COLS)[:-1].tolist()

    def heads(a, n):
        return a.reshape(B, T, n, -1)

    for l in range(DEPTH):
        mod = c @ w_ada[l] + b_ada[l]
        sh1, sc1, gt1, sh2, sc2, gt2 = jnp.split(mod, 6, axis=-1)

        h = rms_norm(x, g_pre_mix[l]) * (1.0 + sc1[:, None]) + sh1[:, None]
        proj = h @ w_in[l]
        (q_n, kc, vc, ksl, vsl, kw, vw, gl,
         q_d, k_d, v_d, qi, ki, wi) = jnp.split(proj, splits, axis=-1)

        o_nsa = nsa_attention(
            rope(heads(q_n, NSA_HEADS), positions),
            rope(heads(kc, NSA_KV), positions), heads(vc, NSA_KV),
            rope(heads(ksl, NSA_KV), positions), heads(vsl, NSA_KV),
            rope(heads(kw, NSA_KV), positions), heads(vw, NSA_KV),
            gl, cmp_pe_k[l], cmp_pe_v[l], cmp_w1_k[l], cmp_w2_k[l], cmp_w1_v[l], cmp_w2_v[l])

        o_dsa = dsa_attention(
            rope(heads(q_d, DSA_HEADS), positions),
            rope(k_d[:, :, None, :], positions)[:, :, 0],
            v_d,
            rope(heads(qi, IDX_HEADS), positions),
            rope(ki[:, :, None, :], positions)[:, :, 0],
            wi)

        o = jnp.concatenate([o_nsa, o_dsa], axis=-1) @ w_out[l]
        x = x + gt1[:, None] * rms_norm(o, g_post_mix[l])

        h = rms_norm(x, g_pre_ffn[l]) * (1.0 + sc2[:, None]) + sh2[:, None]
        y = jnp.square(jax.nn.relu(h @ w_up[l])) @ w_down[l]
        x = x + gt2[:, None] * rms_norm(y, g_post_ffn[l])
    return x
```

```python
import functools

import numpy as np
import jax
import jax.numpy as jnp
from jax import lax
from jax.experimental import pallas as pl
from jax.experimental.pallas import tpu as pltpu

HEAD_DIM = 64
NSA_HEADS = 8
NSA_KV = 2
NSA_REP = NSA_HEADS // NSA_KV
DSA_HEADS = 8
IDX_HEADS = 4
IDX_DIM = 64
CMP_LEN = 32
CMP_STRIDE = 16
SLC_LEN = 64
SLC_TOPN = 16
WIN = 512
DSA_TOPK = 256
ROPE_THETA = 10000.0
EPS = 1e-6
NEG = -1e30
MASK_BIAS = -2e30
M_INIT = -1e30

LANES = 128
MXU_DTYPE = jnp.bfloat16
VMEM_LIMIT = 56 * 1024 * 1024

F32 = jnp.float32
I32 = jnp.int32
INT_MIN = np.int32(-2 ** 31)

_SEG = dict(q_n=(0, 512), kc=(512, 640), vc=(640, 768), ksl=(768, 896), vsl=(896, 1024),
            kw=(1024, 1152), vw=(1152, 1280), gl=(1280, 1304), q_d=(1304, 1816),
            k_d=(1816, 1880), v_d=(1880, 1944), qi=(1944, 2200), ki=(2200, 2264), wi=(2264, 2268))
_ROPE_ORDER = ("q_n", "kc", "ksl", "kw", "q_d", "qi", "k_d", "ki")
_PLAIN_ORDER = ("vc", "vsl", "vw", "v_d", "gl", "wi")
N_ROPE_CHUNKS = 14
D_PROJ = 18 * LANES


def _dot(a, b):
    return jnp.dot(a, b, preferred_element_type=F32)


def _dot_nt(a, b):
    return lax.dot_general(a, b, (((1,), (1,)), ((), ())), preferred_element_type=F32)


def _rms(x, g):
    return x * lax.rsqrt(jnp.mean(x * x, axis=-1, keepdims=True) + EPS) * g


def _ada_kernel(c_ref, w_ref, b_ref, o_ref):
    o_ref[...] = _dot(c_ref[...].astype(MXU_DTYPE), w_ref[...].astype(MXU_DTYPE)) + b_ref[...]


def _ada(c, w, b):
    B, D = c.shape
    n = w.shape[1] // D
    return pl.pallas_call(
        _ada_kernel,
        out_shape=jax.ShapeDtypeStruct((B, n * D), F32),
        grid=(n,),
        in_specs=[pl.BlockSpec((B, D), lambda j: (0, 0)),
                  pl.BlockSpec((D, D), lambda j: (0, j)),
                  pl.BlockSpec((1, D), lambda j: (0, j))],
        out_specs=pl.BlockSpec((B, D), lambda j: (0, j)),
        name="ada",
    )(c, w, b.reshape(1, -1))


def _inproj_kernel(x_ref, mod_ref, g_ref, w_ref, pos_ref, inv_ref,
                   qn_ref, kc_ref, vc_ref, ksl_ref, vsl_ref, kw_ref, vw_ref, gate_ref,
                   qd_ref, kd_ref, vd_ref, qi_ref, ki_ref, wi_ref):
    x = x_ref[0]
    tm = x.shape[0]
    h = _rms(x, g_ref[...]) * (1.0 + mod_ref[0, 1:2, :]) + mod_ref[0, 0:1, :]
    proj = _dot(h.astype(MXU_DTYPE), w_ref[...])

    ang = pos_ref[0] * inv_ref[...]
    lane = lax.broadcasted_iota(I32, (tm, LANES), 1)
    first_half = (lane & (HEAD_DIM // 2)) == 0
    cosv = jnp.cos(ang)
    sinv = jnp.sin(ang)
    sin_signed = jnp.where(first_half, -sinv, sinv)

    def rope(j):
        c = proj[:, j * LANES:(j + 1) * LANES]
        swapped = jnp.where(first_half, pltpu.roll(c, LANES - HEAD_DIM // 2, 1),
                            pltpu.roll(c, HEAD_DIM // 2, 1))
        return c * cosv + swapped * sin_signed

    def plain(j):
        return proj[:, j * LANES:(j + 1) * LANES]

    qscale = HEAD_DIM ** -0.5
    dt = qn_ref.dtype
    for j in range(4):
        r = rope(j) * qscale
        qn_ref[0, 2 * j] = r[:, :HEAD_DIM].astype(dt)
        qn_ref[0, 2 * j + 1] = r[:, HEAD_DIM:].astype(dt)
    for ref, j in ((kc_ref, 4), (ksl_ref, 5), (kw_ref, 6)):
        r = rope(j)
        ref[0, 0] = r[:, :HEAD_DIM].astype(dt)
        ref[0, 1] = r[:, HEAD_DIM:].astype(dt)
    for j in range(4):
        r = rope(7 + j) * qscale
        qd_ref[0, 2 * j] = r[:, :HEAD_DIM].astype(dt)
        qd_ref[0, 2 * j + 1] = r[:, HEAD_DIM:].astype(dt)
    for j in range(2):
        r = rope(11 + j)
        qi_ref[0, 2 * j] = r[:, :HEAD_DIM].astype(dt)
        qi_ref[0, 2 * j + 1] = r[:, HEAD_DIM:].astype(dt)
    r = rope(13)
    kd_ref[0] = r[:, :HEAD_DIM].astype(dt)
    ki_ref[0] = r[:, HEAD_DIM:].astype(dt)
    for ref, j in ((vc_ref, 14), (vsl_ref, 15), (vw_ref, 16)):
        v = plain(j)
        ref[0, 0] = v[:, :HEAD_DIM].astype(dt)
        ref[0, 1] = v[:, HEAD_DIM:].astype(dt)
    last = plain(17)
    vd_ref[0] = last[:, :HEAD_DIM].astype(dt)
    ng = NSA_REP * 3
    gates = jax.nn.sigmoid(last[:, HEAD_DIM:HEAD_DIM + 2 * ng])
    gate_ref[0, 0] = gates[:, :ng]
    gate_ref[0, 1] = gates[:, ng:]
    idx_scale = (IDX_HEADS ** -0.5) * (IDX_DIM ** -0.5)
    wi_ref[0] = last[:, HEAD_DIM + 2 * ng:HEAD_DIM + 2 * ng + IDX_HEADS] * idx_scale


def _in_proj(x, mod, g_pre, w_perm, pos_f, inv_tile, tm):
    B, T, D = x.shape
    dt = MXU_DTYPE
    hd = HEAD_DIM

    def heads(n):
        return (jax.ShapeDtypeStruct((B, n, T, hd), dt),
                pl.BlockSpec((1, n, tm, hd), lambda b, i: (b, 0, i, 0)))

    def flat(w, d=dt):
        return (jax.ShapeDtypeStruct((B, T, w), d),
                pl.BlockSpec((1, tm, w), lambda b, i: (b, i, 0)))

    outs = [heads(NSA_HEADS),
            heads(NSA_KV), heads(NSA_KV),
            heads(NSA_KV), heads(NSA_KV),
            heads(NSA_KV), heads(NSA_KV),
            (jax.ShapeDtypeStruct((B, NSA_KV, T, NSA_REP * 3), F32),
             pl.BlockSpec((1, NSA_KV, tm, NSA_REP * 3), lambda b, i: (b, 0, i, 0))),
            heads(DSA_HEADS),
            flat(hd), flat(hd),
            heads(IDX_HEADS),
            flat(hd),
            flat(IDX_HEADS, F32)]
    return pl.pallas_call(
        _inproj_kernel,
        out_shape=[o[0] for o in outs],
        grid=(B, T // tm),
        in_specs=[pl.BlockSpec((1, tm, D), lambda b, i: (b, i, 0)),
                  pl.BlockSpec((1, 6, D), lambda b, i: (b, 0, 0)),
                  pl.BlockSpec((1, D), lambda b, i: (0, 0)),
                  pl.BlockSpec((D, D_PROJ), lambda b, i: (0, 0)),
                  pl.BlockSpec((1, tm, 1), lambda b, i: (b, i, 0)),
                  pl.BlockSpec((1, LANES), lambda b, i: (0, 0))],
        out_specs=[o[1] for o in outs],
        compiler_params=pltpu.CompilerParams(
            dimension_semantics=("parallel", "parallel"), vmem_limit_bytes=VMEM_LIMIT),
        name="in_proj",
    )(x, mod, g_pre, w_perm, pos_f, inv_tile)


def _softmax_step(s, m_prev, l_prev, acc_prev, v):
    m_new = jnp.maximum(m_prev, jnp.max(s, axis=-1, keepdims=True))
    alpha = jnp.exp(m_prev - m_new)
    p = jnp.exp(s - m_new)
    l_new = alpha * l_prev + jnp.sum(p, axis=-1, keepdims=True)
    acc_new = alpha * acc_prev + _dot(p.astype(v.dtype), v)
    return m_new, l_new, acc_new


def _nsa_kernel(q_ref, kc_ref, vc_ref, ksl_ref, vsl_ref, kw_ref, vw_ref, gate_ref,
                pek_ref, pev_ref, w1k_ref, w2k_ref, w1v_ref, w2v_ref, e_ref, ovl_ref,
                o_ref, kcmp_ref, vcmp_ref, *, tq, tk, seq):
    qi = pl.program_id(2)
    s0 = qi * tq
    ncmp = seq // CMP_STRIDE
    nslc = seq // SLC_LEN
    nsel = min(SLC_TOPN, nslc)
    rows = NSA_REP * tq
    hd = HEAD_DIM

    @pl.when(qi == 0)
    def _():
        half = CMP_STRIDE * hd

        def compress(r_ref, pe_ref, w1_ref, w2_ref, out_ref):
            r = r_ref[0, 0]
            a = _dot(r, w1_ref[:half, :])
            b = _dot(r, w1_ref[half:, :])
            pe = jnp.broadcast_to(pe_ref[...], (8, CMP_LEN * hd)).astype(MXU_DTYPE)
            bias = _dot(pe, w1_ref[...])[0:1, :]
            pre = a + pltpu.roll(b, ncmp - 1, 0) + bias
            out_ref[...] = _dot(jax.nn.gelu(pre).astype(MXU_DTYPE), w2_ref[...]).astype(out_ref.dtype)

        compress(kc_ref, pek_ref, w1k_ref, w2k_ref, kcmp_ref)
        compress(vc_ref, pev_ref, w1v_ref, w2v_ref, vcmp_ref)

    q = q_ref[0].reshape(rows, hd)

    s_c = _dot_nt(q, kcmp_ref[...]).reshape(NSA_REP, tq, ncmp)
    t_c = s0 + lax.broadcasted_iota(I32, (1, tq, ncmp), 1)
    n_c = lax.broadcasted_iota(I32, (1, tq, ncmp), 2)
    mask_c = (n_c * CMP_STRIDE + (CMP_LEN - 1)) <= t_c
    s_c = jnp.where(mask_c, s_c, NEG)
    p_c = jnp.exp(s_c - jnp.max(s_c, axis=-1, keepdims=True)) * mask_c.astype(F32)
    l_c = jnp.sum(p_c, axis=-1, keepdims=True)
    p_c = p_c / jnp.maximum(l_c, 1e-30)
    o_cmp = _dot(p_c.reshape(rows, ncmp).astype(MXU_DTYPE), vcmp_ref[...])

    p_sum = jnp.sum(p_c, axis=0)
    p_hi = p_sum.astype(MXU_DTYPE)
    p_lo = (p_sum - p_hi.astype(F32)).astype(MXU_DTYPE)
    imp = _dot(p_hi, ovl_ref[...]) + _dot(p_lo, ovl_ref[...])
    j_idx = lax.broadcasted_iota(I32, (tq, nslc), 1)
    t_s = s0 + lax.broadcasted_iota(I32, (tq, nslc), 0)
    cur = lax.shift_right_logical(t_s, int(np.log2(SLC_LEN)))
    valid = j_idx <= cur
    forced = (j_idx == 0) | (j_idx == cur) | (j_idx == cur - 1)
    sc = jnp.where(valid, jnp.where(forced, jnp.inf, imp), -jnp.inf)
    rank = jnp.zeros((tq, nslc), I32)
    for i in range(nslc):
        ci = sc[:, i:i + 1]
        ahead = (ci > sc) | ((ci == sc) & (j_idx > i))
        rank = rank + jnp.where(ahead, 1, 0)
    selected = (rank < nsel) & valid
    blk_bias = jnp.where(selected, 0.0, MASK_BIAS).astype(MXU_DTYPE)

    def sel_body(kt, carry):
        m, l, acc = carry
        off = pl.multiple_of(kt * tk, tk)
        k = ksl_ref[0, 0, pl.ds(off, tk), :]
        v = vsl_ref[0, 0, pl.ds(off, tk), :]
        bias = _dot(blk_bias, e_ref[kt])
        kpos = off + lax.broadcasted_iota(I32, (tq, tk), 1)
        tt = s0 + lax.broadcasted_iota(I32, (tq, tk), 0)
        bias = jnp.where(kpos <= tt, bias, MASK_BIAS)
        s = (_dot_nt(q, k).reshape(NSA_REP, tq, tk) + bias[None]).reshape(rows, tk)
        return _softmax_step(s, m, l, acc, v)

    init = (jnp.full((rows, 1), M_INIT, F32), jnp.zeros((rows, 1), F32), jnp.zeros((rows, hd), F32))
    n_kt = s0 // tk + 1
    _, l_s, acc_s = lax.fori_loop(0, n_kt, sel_body, init)
    o_slc = acc_s / l_s

    span = WIN + tq
    w0 = pl.multiple_of(jnp.maximum(s0 - WIN, 0), tq)
    kwv = kw_ref[0, 0, pl.ds(w0, span), :]
    vwv = vw_ref[0, 0, pl.ds(w0, span), :]
    kpos = w0 + lax.broadcasted_iota(I32, (tq, span), 1)
    tt = s0 + lax.broadcasted_iota(I32, (tq, span), 0)
    bias_w = jnp.where((kpos <= tt) & (kpos > tt - WIN), 0.0, MASK_BIAS)
    s_w = (_dot_nt(q, kwv).reshape(NSA_REP, tq, span) + bias_w[None]).reshape(rows, span)
    m_w = jnp.max(s_w, axis=-1, keepdims=True)
    p_w = jnp.exp(s_w - m_w)
    o_win = _dot(p_w.astype(MXU_DTYPE), vwv) / jnp.sum(p_w, axis=-1, keepdims=True)

    g = gate_ref[0, 0]
    outs = []
    for r in range(NSA_REP):
        sl = slice(r * tq, (r + 1) * tq)
        outs.append(g[:, 3 * r:3 * r + 1] * o_cmp[sl]
                    + g[:, 3 * r + 1:3 * r + 2] * o_slc[sl]
                    + g[:, 3 * r + 2:3 * r + 3] * o_win[sl])
    o_ref[0] = jnp.concatenate(outs, axis=-1).astype(o_ref.dtype)


def _nsa(qn, kc_rows, vc_rows, ksl, vsl, kw, vw, gates, pe_k, pe_v, w1k, w2k, w1v, w2v, tq, tk):
    B, _, T, hd = qn.shape
    ncmp = T // CMP_STRIDE
    nslc = T // SLC_LEN
    key = np.arange(T)
    e = (key[None, :] // SLC_LEN == np.arange(nslc)[:, None]).astype(np.float32)
    e = jnp.asarray(e.reshape(nslc, T // tk, tk).transpose(1, 0, 2), MXU_DTYPE)
    cmp_start = np.arange(ncmp) * CMP_STRIDE
    cmp_end = cmp_start + CMP_LEN - 1
    slc_start = np.arange(nslc) * SLC_LEN
    slc_end = slc_start + SLC_LEN - 1
    ovl = ((cmp_start[:, None] <= slc_end[None, :]) & (cmp_end[:, None] >= slc_start[None, :]))
    ovl[ncmp - 1, :] = False
    ovl = jnp.asarray(ovl.astype(np.float32), MXU_DTYPE)

    kv_spec = pl.BlockSpec((1, 1, T, hd), lambda b, g, i: (b, g, 0, 0))
    row_spec = pl.BlockSpec((1, 1, ncmp, CMP_STRIDE * hd), lambda b, g, i: (b, g, 0, 0))

    def whole(a):
        return pl.BlockSpec(a.shape, lambda b, g, i: (0,) * a.ndim)

    consts = (pe_k, pe_v, w1k, w2k, w1v, w2v, e, ovl)
    return pl.pallas_call(
        functools.partial(_nsa_kernel, tq=tq, tk=tk, seq=T),
        out_shape=jax.ShapeDtypeStruct((B, T, NSA_HEADS * hd), MXU_DTYPE),
        grid=(B, NSA_KV, T // tq),
        in_specs=[pl.BlockSpec((1, NSA_REP, tq, hd), lambda b, g, i: (b, g, i, 0)),
                  row_spec, row_spec, kv_spec, kv_spec, kv_spec, kv_spec,
                  pl.BlockSpec((1, 1, tq, NSA_REP * 3), lambda b, g, i: (b, g, i, 0))]
                 + [whole(a) for a in consts],
        out_specs=pl.BlockSpec((1, tq, NSA_REP * hd), lambda b, g, i: (b, i, g)),
        scratch_shapes=[pltpu.VMEM((ncmp, hd), MXU_DTYPE), pltpu.VMEM((ncmp, hd), MXU_DTYPE)],
        compiler_params=pltpu.CompilerParams(
            dimension_semantics=("parallel", "parallel", "arbitrary"), vmem_limit_bytes=VMEM_LIMIT),
        name="nsa",
    )(qn, kc_rows, vc_rows, ksl, vsl, kw, vw, gates, *consts)


def _dsa_kernel(qd_ref, qi_ref, wi_ref, kd_ref, vd_ref, ki_ref, tri_ref, o_ref, key_ref,
                *, tq, ck, ktop):
    s0 = pl.program_id(1) * tq
    n_kc = s0 // ck + 1
    hd = HEAD_DIM
    rows = DSA_HEADS * tq
    sub = ck // LANES

    q_idx = qi_ref[0].reshape(IDX_HEADS * tq, IDX_DIM)
    w = wi_ref[0]
    tt = s0 + lax.broadcasted_iota(I32, (tq, ck), 0)
    col = lax.broadcasted_iota(I32, (tq, ck), 1)

    def score_body(c, carry):
        off = pl.multiple_of(c * ck, ck)
        lg = _dot_nt(q_idx, ki_ref[0, pl.ds(off, ck), :]).reshape(IDX_HEADS, tq, ck)
        sc = w[:, 0:1] * jnp.maximum(lg[0], 0.0)
        for h in range(1, IDX_HEADS):
            sc = sc + w[:, h:h + 1] * jnp.maximum(lg[h], 0.0)
        sc = jnp.where(sc == 0.0, 0.0, sc)
        bits = pltpu.bitcast(sc, I32)
        key = jnp.where(bits < 0, bits ^ np.int32(0x7FFFFFFF), bits)
        key_ref[c] = jnp.where(off + col <= tt, key, INT_MIN)
        return carry

    lax.fori_loop(0, n_kc, score_body, 0)

    def count_ge(cand):
        cand_b = jnp.broadcast_to(cand, (tq, LANES))

        def body(c, cnt):
            k = key_ref[c]
            for s in range(sub):
                cnt = cnt + jnp.where(k[:, s * LANES:(s + 1) * LANES] >= cand_b, 1, 0)
            return cnt

        cnt = lax.fori_loop(0, n_kc, body, jnp.zeros((tq, LANES), I32))
        return jnp.sum(cnt, axis=-1, keepdims=True)

    base = jnp.where(count_ge(jnp.zeros((tq, 1), I32)) >= ktop, np.int32(0), INT_MIN)

    def bit_body(it, base):
        cand = base | jnp.left_shift(np.int32(1), np.int32(30) - it)
        return jnp.where(count_ge(cand) >= ktop, cand, base)

    thr = lax.fori_loop(0, 31, bit_body, base)
    need_eq = (ktop - count_ge(thr + 1)).astype(F32)

    q = qd_ref[0].reshape(rows, hd)

    def att_body(c, carry):
        m, l, acc, eq_seen = carry
        off = pl.multiple_of(c * ck, ck)
        key = key_ref[c]
        eq = key == thr
        eq_f = jnp.where(eq, 1.0, 0.0)
        before = _dot(eq_f.astype(MXU_DTYPE), tri_ref[...]) + eq_seen
        sel = ((key > thr) | (eq & (before < need_eq))) & (off + col <= tt)
        bias = jnp.where(sel, 0.0, MASK_BIAS)
        k = kd_ref[0, pl.ds(off, ck), :]
        v = vd_ref[0, pl.ds(off, ck), :]
        s = (_dot_nt(q, k).reshape(DSA_HEADS, tq, ck) + bias[None]).reshape(rows, ck)
        m, l, acc = _softmax_step(s, m, l, acc, v)
        return m, l, acc, eq_seen + jnp.sum(eq_f, axis=-1, keepdims=True)

    init = (jnp.full((rows, 1), M_INIT, F32), jnp.zeros((rows, 1), F32),
            jnp.zeros((rows, hd), F32), jnp.zeros((tq, 1), F32))
    _, l, acc, _ = lax.fori_loop(0, n_kc, att_body, init)
    o = acc / l
    o_ref[0] = jnp.concatenate([o[h * tq:(h + 1) * tq] for h in range(DSA_HEADS)],
                               axis=-1).astype(o_ref.dtype)


def _dsa(qd, qi, wi, kd, vd, ki, tq, ck):
    B, _, T, hd = qd.shape
    ktop = min(DSA_TOPK, T // 4)
    tri = jnp.asarray(np.triu(np.ones((ck, ck), np.float32), 1), MXU_DTYPE)
    kv_spec = pl.BlockSpec((1, T, hd), lambda b, i: (b, 0, 0))
    return pl.pallas_call(
        functools.partial(_dsa_kernel, tq=tq, ck=ck, ktop=ktop),
        out_shape=jax.ShapeDtypeStruct((B, T, DSA_HEADS * hd), MXU_DTYPE),
        grid=(B, T // tq),
        in_specs=[pl.BlockSpec((1, DSA_HEADS, tq, hd), lambda b, i: (b, 0, i, 0)),
                  pl.BlockSpec((1, IDX_HEADS, tq, IDX_DIM), lambda b, i: (b, 0, i, 0)),
                  pl.BlockSpec((1, tq, IDX_HEADS), lambda b, i: (b, i, 0)),
                  kv_spec, kv_spec, kv_spec,
                  pl.BlockSpec((ck, ck), lambda b, i: (0, 0))],
        out_specs=pl.BlockSpec((1, tq, DSA_HEADS * hd), lambda b, i: (b, i, 0)),
        scratch_shapes=[pltpu.VMEM((T // ck, tq, ck), I32)],
        compiler_params=pltpu.CompilerParams(
            dimension_semantics=("parallel", "parallel"), vmem_limit_bytes=VMEM_LIMIT),
        name="dsa",
    )(qd, qi, wi, kd, vd, ki, tri)


def _ffn_kernel(on_ref, od_ref, x_ref, mod_ref, gpm_ref, gpf_ref, gpo_ref,
                woa_ref, wob_ref, wup_ref, wdn_ref, out_ref, *, fc):
    o = _dot(on_ref[0], woa_ref[...]) + _dot(od_ref[0], wob_ref[...])
    x1 = x_ref[0] + mod_ref[0, 2:3, :] * _rms(o, gpm_ref[...])
    h = _rms(x1, gpf_ref[...]) * (1.0 + mod_ref[0, 4:5, :]) + mod_ref[0, 3:4, :]
    hb = h.astype(MXU_DTYPE)
    d_ff = wup_ref.shape[1]
    y = jnp.zeros(x1.shape, F32)
    for c in range(d_ff // fc):
        u = jnp.maximum(_dot(hb, wup_ref[:, c * fc:(c + 1) * fc]), 0.0)
        y = y + _dot((u * u).astype(MXU_DTYPE), wdn_ref[c * fc:(c + 1) * fc, :])
    out_ref[0] = x1 + mod_ref[0, 5:6, :] * _rms(y, gpo_ref[...])


def _ffn(o_nsa, o_dsa, x, mod, g_post_mix, g_pre_ffn, g_post_ffn, w_out, w_up, w_down, tm, fc):
    B, T, D = x.shape
    dn = o_nsa.shape[-1]

    def tok(w):
        return pl.BlockSpec((1, tm, w), lambda b, i: (b, i, 0))

    def whole(a):
        return pl.BlockSpec(a.shape, lambda b, i: (0,) * a.ndim, pipeline_mode=pl.Buffered(1))

    woa, wob = w_out[:dn], w_out[dn:]
    vec = pl.BlockSpec((1, D), lambda b, i: (0, 0))
    return pl.pallas_call(
        functools.partial(_ffn_kernel, fc=fc),
        out_shape=jax.ShapeDtypeStruct((B, T, D), F32),
        grid=(B, T // tm),
        in_specs=[tok(dn), tok(o_dsa.shape[-1]), tok(D),
                  pl.BlockSpec((1, 6, D), lambda b, i: (b, 0, 0)), vec, vec, vec,
                  whole(woa), whole(wob), whole(w_up), whole(w_down)],
        out_specs=tok(D),
        compiler_params=pltpu.CompilerParams(
            dimension_semantics=("parallel", "parallel"), vmem_limit_bytes=VMEM_LIMIT),
        name="ffn",
    )(o_nsa, o_dsa, x, mod, g_post_mix, g_pre_ffn, g_post_ffn, woa, wob, w_up, w_down)


def _permute_w_in(w):
    cols = [w[:, _SEG[n][0]:_SEG[n][1]] for n in _ROPE_ORDER + _PLAIN_ORDER]
    used = sum(c.shape[1] for c in cols)
    cols.append(jnp.zeros((w.shape[0], D_PROJ - used), w.dtype))
    return jnp.concatenate(cols, axis=1).astype(MXU_DTYPE)


def kernel(x, c, positions, w_ada, b_ada, g_pre_mix, g_post_mix, g_pre_ffn, g_post_ffn,
           w_in, cmp_pe_k, cmp_pe_v, cmp_w1_k, cmp_w2_k, cmp_w1_v, cmp_w2_v,
           w_out, w_up, w_down):
    B, T, D = x.shape
    depth = w_ada.shape[0]
    tm = min(512, T)
    half = HEAD_DIM // 2
    inv = ROPE_THETA ** (-jnp.arange(half, dtype=F32) / half)
    inv_tile = jnp.tile(inv, LANES // half).reshape(1, LANES)
    pos_f = positions.astype(F32).reshape(B, T, 1)
    ncmp = T // CMP_STRIDE

    for l in range(depth):
        mod = _ada(c, w_ada[l], b_ada[l]).reshape(B, 6, D)
        (qn, kc, vc, ksl, vsl, kw, vw, gates, qd, kd, vd, qi, ki, wi) = _in_proj(
            x, mod, g_pre_mix[l].reshape(1, D), _permute_w_in(w_in[l]), pos_f, inv_tile, tm)
        rows = (B, NSA_KV, ncmp, CMP_STRIDE * HEAD_DIM)
        o_nsa = _nsa(qn, kc.reshape(rows), vc.reshape(rows), ksl, vsl, kw, vw, gates,
                     cmp_pe_k[l].reshape(1, -1), cmp_pe_v[l].reshape(1, -1),
                     cmp_w1_k[l].astype(MXU_DTYPE), cmp_w2_k[l].astype(MXU_DTYPE),
                     cmp_w1_v[l].astype(MXU_DTYPE), cmp_w2_v[l].astype(MXU_DTYPE),
                     tq=128, tk=min(512, T))
        o_dsa = _dsa(qd, qi, wi, kd, vd, ki, tq=128, ck=min(512, T))
        x = _ffn(o_nsa, o_dsa, x, mod, g_post_mix[l].reshape(1, D), g_pre_ffn[l].reshape(1, D),
                 g_post_ffn[l].reshape(1, D), w_out[l].astype(MXU_DTYPE),
                 w_up[l].astype(MXU_DTYPE), w_down[l].astype(MXU_DTYPE), tm=tm, fc=1024)
    return x
```

```python
import functools

import numpy as np
import jax
import jax.numpy as jnp
from jax import lax
from jax.experimental import pallas as pl
from jax.experimental.pallas import tpu as pltpu

HEAD_DIM = 64
NSA_HEADS = 8
NSA_KV = 2
NSA_REP = NSA_HEADS // NSA_KV
DSA_HEADS = 8
IDX_HEADS = 4
IDX_DIM = 64
CMP_LEN = 32
CMP_STRIDE = 16
SLC_LEN = 64
SLC_TOPN = 16
WIN = 512
DSA_TOPK = 256
ROPE_THETA = 10000.0
EPS = 1e-6
NEG = -1e30
MASK_BIAS = -2e30
M_INIT = -1e30

LANES = 128
SUBLANES = 8
MXU_DTYPE = jnp.bfloat16
VMEM_LIMIT = 56 * 1024 * 1024

TQ = LANES
F32 = jnp.float32
I32 = jnp.int32
INT_MIN = np.int32(-2 ** 31)

_SEG = dict(q_n=(0, 512), kc=(512, 640), vc=(640, 768), ksl=(768, 896), vsl=(896, 1024),
            kw=(1024, 1152), vw=(1152, 1280), gl=(1280, 1304), q_d=(1304, 1816),
            k_d=(1816, 1880), v_d=(1880, 1944), qi=(1944, 2200), ki=(2200, 2264), wi=(2264, 2268))
_ROPE_ORDER = ("q_n", "kc", "ksl", "kw", "q_d", "qi", "k_d", "ki")
_PLAIN_ORDER = ("vc", "vsl", "vw", "v_d", "gl", "wi")
D_PROJ = 18 * LANES


def _dot(a, b):
    return jnp.dot(a, b, preferred_element_type=F32)


def _dot_nt(a, b):
    return lax.dot_general(a, b, (((1,), (1,)), ((), ())), preferred_element_type=F32)


def _rms(x, g):
    return x * lax.rsqrt(jnp.mean(x * x, axis=-1, keepdims=True) + EPS) * g


def _ada_kernel(c_ref, w_ref, b_ref, o_ref):
    o_ref[...] = _dot(c_ref[...].astype(MXU_DTYPE), w_ref[...].astype(MXU_DTYPE)) + b_ref[...]


def _ada(c, w, b):
    B, D = c.shape
    n = w.shape[1] // D
    return pl.pallas_call(
        _ada_kernel,
        out_shape=jax.ShapeDtypeStruct((B, n * D), F32),
        grid=(n,),
        in_specs=[pl.BlockSpec((B, D), lambda j: (0, 0)),
                  pl.BlockSpec((D, D), lambda j: (0, j)),
                  pl.BlockSpec((1, D), lambda j: (0, j))],
        out_specs=pl.BlockSpec((B, D), lambda j: (0, j)),
        name="ada",
    )(c, w, b.reshape(1, -1))


def _inproj_kernel(x_ref, mod_ref, g_ref, w_ref, pos_ref, inv_ref,
                   qn_ref, kc_ref, vc_ref, ksl_ref, vsl_ref, kw_ref, vw_ref, gate_ref,
                   qd_ref, kd_ref, vd_ref, qi_ref, ki_ref, wi_ref):
    x = x_ref[0]
    tm = x.shape[0]
    h = _rms(x, g_ref[...]) * (1.0 + mod_ref[0, 1:2, :]) + mod_ref[0, 0:1, :]
    proj = _dot(h.astype(MXU_DTYPE), w_ref[...])

    ang = pos_ref[0] * inv_ref[...]
    lane = lax.broadcasted_iota(I32, (tm, LANES), 1)
    first_half = (lane & (HEAD_DIM // 2)) == 0
    cosv = jnp.cos(ang)
    sinv = jnp.sin(ang)
    sin_signed = jnp.where(first_half, -sinv, sinv)

    def rope(j):
        c = proj[:, j * LANES:(j + 1) * LANES]
        swapped = jnp.where(first_half, pltpu.roll(c, LANES - HEAD_DIM // 2, 1),
                            pltpu.roll(c, HEAD_DIM // 2, 1))
        return c * cosv + swapped * sin_signed

    def plain(j):
        return proj[:, j * LANES:(j + 1) * LANES]

    qscale = HEAD_DIM ** -0.5
    dt = qn_ref.dtype
    for j in range(4):
        r = rope(j) * qscale
        qn_ref[0, 2 * j] = r[:, :HEAD_DIM].astype(dt)
        qn_ref[0, 2 * j + 1] = r[:, HEAD_DIM:].astype(dt)
    for ref, j in ((kc_ref, 4), (ksl_ref, 5), (kw_ref, 6)):
        r = rope(j)
        ref[0, 0] = r[:, :HEAD_DIM].astype(dt)
        ref[0, 1] = r[:, HEAD_DIM:].astype(dt)
    for j in range(4):
        r = rope(7 + j) * qscale
        qd_ref[0, 2 * j] = r[:, :HEAD_DIM].astype(dt)
        qd_ref[0, 2 * j + 1] = r[:, HEAD_DIM:].astype(dt)
    for j in range(2):
        r = rope(11 + j)
        qi_ref[0, 2 * j] = r[:, :HEAD_DIM].astype(dt)
        qi_ref[0, 2 * j + 1] = r[:, HEAD_DIM:].astype(dt)
    r = rope(13)
    kd_ref[0] = r[:, :HEAD_DIM].astype(dt)
    ki_ref[0] = r[:, HEAD_DIM:].astype(dt)
    for ref, j in ((vc_ref, 14), (vsl_ref, 15), (vw_ref, 16)):
        v = plain(j)
        ref[0, 0] = v[:, :HEAD_DIM].astype(dt)
        ref[0, 1] = v[:, HEAD_DIM:].astype(dt)
    last = plain(17)
    vd_ref[0] = last[:, :HEAD_DIM].astype(dt)
    ng = NSA_REP * 3
    gates = jax.nn.sigmoid(last[:, HEAD_DIM:HEAD_DIM + 2 * ng])
    gate_ref[0, 0] = gates[:, :ng]
    gate_ref[0, 1] = gates[:, ng:]
    idx_scale = (IDX_HEADS ** -0.5) * (IDX_DIM ** -0.5)
    wi_ref[0] = last[:, HEAD_DIM + 2 * ng:HEAD_DIM + 2 * ng + IDX_HEADS] * idx_scale


def _in_proj(x, mod, g_pre, w_perm, pos_f, inv_tile, tm):
    B, T, D = x.shape
    dt = MXU_DTYPE
    hd = HEAD_DIM

    def heads(n):
        return (jax.ShapeDtypeStruct((B, n, T, hd), dt),
                pl.BlockSpec((1, n, tm, hd), lambda b, i: (b, 0, i, 0)))

    def flat(w, d=dt):
        return (jax.ShapeDtypeStruct((B, T, w), d),
                pl.BlockSpec((1, tm, w), lambda b, i: (b, i, 0)))

    outs = [heads(NSA_HEADS),
            heads(NSA_KV), heads(NSA_KV),
            heads(NSA_KV), heads(NSA_KV),
            heads(NSA_KV), heads(NSA_KV),
            (jax.ShapeDtypeStruct((B, NSA_KV, T, NSA_REP * 3), F32),
             pl.BlockSpec((1, NSA_KV, tm, NSA_REP * 3), lambda b, i: (b, 0, i, 0))),
            heads(DSA_HEADS),
            flat(hd), flat(hd),
            heads(IDX_HEADS),
            flat(hd),
            flat(IDX_HEADS, F32)]
    return pl.pallas_call(
        _inproj_kernel,
        out_shape=[o[0] for o in outs],
        grid=(B, T // tm),
        in_specs=[pl.BlockSpec((1, tm, D), lambda b, i: (b, i, 0)),
                  pl.BlockSpec((1, 6, D), lambda b, i: (b, 0, 0)),
                  pl.BlockSpec((1, D), lambda b, i: (0, 0)),
                  pl.BlockSpec((D, D_PROJ), lambda b, i: (0, 0)),
                  pl.BlockSpec((1, tm, 1), lambda b, i: (b, i, 0)),
                  pl.BlockSpec((1, LANES), lambda b, i: (0, 0))],
        out_specs=[o[1] for o in outs],
        compiler_params=pltpu.CompilerParams(
            dimension_semantics=("parallel", "parallel"), vmem_limit_bytes=VMEM_LIMIT),
        name="in_proj",
    )(x, mod, g_pre, w_perm, pos_f, inv_tile)


def _softmax_step(s, m_prev, l_prev, acc_prev, v):
    m_new = jnp.maximum(m_prev, jnp.max(s, axis=-1, keepdims=True))
    alpha = jnp.exp(m_prev - m_new)
    p = jnp.exp(s - m_new)
    l_new = alpha * l_prev + jnp.sum(p, axis=-1, keepdims=True)
    acc_new = alpha * acc_prev + _dot(p.astype(v.dtype), v)
    return m_new, l_new, acc_new


def _nsa_kernel(q_ref, kc_ref, vc_ref, ksl_ref, vsl_ref, kw_ref, vw_ref, gate_ref,
                pek_ref, pev_ref, w1k_ref, w2k_ref, w1v_ref, w2v_ref, e_ref, ovlt_ref,
                o_ref, kcmp_ref, vcmp_ref, *, tk, seq):
    tq = TQ
    qi = pl.program_id(2)
    s0 = qi * tq
    ncmp = seq // CMP_STRIDE
    nslc = seq // SLC_LEN
    nsel = min(SLC_TOPN, nslc)
    rows = NSA_REP * tq
    hd = HEAD_DIM

    @pl.when(qi == 0)
    def _():
        half = CMP_STRIDE * hd

        def compress(r_ref, pe_ref, w1_ref, w2_ref, out_ref):
            r = r_ref[0, 0]
            a = _dot(r, w1_ref[:half, :])
            b = _dot(r, w1_ref[half:, :])
            pe = jnp.broadcast_to(pe_ref[...], (SUBLANES, CMP_LEN * hd)).astype(MXU_DTYPE)
            bias = _dot(pe, w1_ref[...])[0:1, :]
            pre = a + pltpu.roll(b, ncmp - 1, 0) + bias
            out_ref[...] = _dot(jax.nn.gelu(pre).astype(MXU_DTYPE), w2_ref[...]).astype(out_ref.dtype)

        compress(kc_ref, pek_ref, w1k_ref, w2k_ref, kcmp_ref)
        compress(vc_ref, pev_ref, w1v_ref, w2v_ref, vcmp_ref)

    q = q_ref[0].reshape(rows, hd)

    s_c = _dot_nt(q, kcmp_ref[...]).reshape(NSA_REP, tq, ncmp)
    t_c = s0 + lax.broadcasted_iota(I32, (1, tq, ncmp), 1)
    n_c = lax.broadcasted_iota(I32, (1, tq, ncmp), 2)
    mask_c = (n_c * CMP_STRIDE + (CMP_LEN - 1)) <= t_c
    s_c = jnp.where(mask_c, s_c, NEG)
    p_c = jnp.exp(s_c - jnp.max(s_c, axis=-1, keepdims=True)) * jnp.where(mask_c, 1.0, 0.0)
    l_c = jnp.sum(p_c, axis=-1, keepdims=True)
    p_c = p_c * (1.0 / jnp.maximum(l_c, 1e-30))
    o_cmp = _dot(p_c.reshape(rows, ncmp).astype(MXU_DTYPE), vcmp_ref[...])

    p_sum = jnp.sum(p_c, axis=0)
    p_hi = p_sum.astype(MXU_DTYPE)
    p_lo = (p_sum - p_hi.astype(F32)).astype(MXU_DTYPE)
    imp = _dot_nt(ovlt_ref[...], p_hi) + _dot_nt(ovlt_ref[...], p_lo)
    j_idx = lax.broadcasted_iota(I32, (nslc, tq), 0)
    t_s = s0 + lax.broadcasted_iota(I32, (nslc, tq), 1)
    cur = lax.shift_right_logical(t_s, int(np.log2(SLC_LEN)))
    sc = jnp.where(j_idx == 0, jnp.inf,
                   jnp.where(j_idx == cur, jnp.inf, jnp.where(j_idx == cur - 1, jnp.inf, imp)))
    sc = jnp.where(j_idx <= cur, sc, -jnp.inf)
    rank = jnp.zeros((nslc, tq), I32)
    for i in range(nslc):
        ci = sc[i:i + 1, :]
        rank = rank + jnp.where(ci > sc, 1, jnp.where(ci == sc, jnp.where(j_idx > i, 1, 0), 0))
    blk_bias = jnp.where(rank < nsel, jnp.where(j_idx <= cur, 0.0, MASK_BIAS), MASK_BIAS)
    blk_bias = blk_bias.T.astype(MXU_DTYPE)

    def sel_body(kt, carry):
        m, l, acc = carry
        off = pl.multiple_of(kt * tk, tk)
        k = ksl_ref[0, 0, pl.ds(off, tk), :]
        v = vsl_ref[0, 0, pl.ds(off, tk), :]
        bias = _dot(blk_bias, e_ref[kt])
        kpos = off + lax.broadcasted_iota(I32, (tq, tk), 1)
        tt = s0 + lax.broadcasted_iota(I32, (tq, tk), 0)
        bias = jnp.where(kpos <= tt, bias, MASK_BIAS)
        s = (_dot_nt(q, k).reshape(NSA_REP, tq, tk) + bias[None]).reshape(rows, tk)
        return _softmax_step(s, m, l, acc, v)

    init = (jnp.full((rows, 1), M_INIT, F32), jnp.zeros((rows, 1), F32), jnp.zeros((rows, hd), F32))
    _, l_s, acc_s = lax.fori_loop(0, s0 // tk + 1, sel_body, init)
    o_slc = acc_s * (1.0 / l_s)

    span = WIN + tq
    w0 = pl.multiple_of(jnp.maximum(s0 - WIN, 0), tq)
    kwv = kw_ref[0, 0, pl.ds(w0, span), :]
    vwv = vw_ref[0, 0, pl.ds(w0, span), :]
    kpos = w0 + lax.broadcasted_iota(I32, (tq, span), 1)
    tt = s0 + lax.broadcasted_iota(I32, (tq, span), 0)
    bias_w = jnp.where(kpos <= tt, jnp.where(kpos > tt - WIN, 0.0, MASK_BIAS), MASK_BIAS)
    s_w = (_dot_nt(q, kwv).reshape(NSA_REP, tq, span) + bias_w[None]).reshape(rows, span)
    p_w = jnp.exp(s_w - jnp.max(s_w, axis=-1, keepdims=True))
    o_win = _dot(p_w.astype(MXU_DTYPE), vwv) * (1.0 / jnp.sum(p_w, axis=-1, keepdims=True))

    g = gate_ref[0, 0]
    outs = []
    for r in range(NSA_REP):
        sl = slice(r * tq, (r + 1) * tq)
        outs.append(g[:, 3 * r:3 * r + 1] * o_cmp[sl]
                    + g[:, 3 * r + 1:3 * r + 2] * o_slc[sl]
                    + g[:, 3 * r + 2:3 * r + 3] * o_win[sl])
    o_ref[0] = jnp.concatenate(outs, axis=-1).astype(o_ref.dtype)


def _nsa(qn, kc_rows, vc_rows, ksl, vsl, kw, vw, gates, pe_k, pe_v, w1k, w2k, w1v, w2v, tk):
    B, _, T, hd = qn.shape
    tq = TQ
    ncmp = T // CMP_STRIDE
    nslc = T // SLC_LEN
    key = np.arange(T)
    e = (key[None, :] // SLC_LEN == np.arange(nslc)[:, None]).astype(np.float32)
    e = jnp.asarray(e.reshape(nslc, T // tk, tk).transpose(1, 0, 2), MXU_DTYPE)
    cmp_start = np.arange(ncmp) * CMP_STRIDE
    cmp_end = cmp_start + CMP_LEN - 1
    slc_start = np.arange(nslc) * SLC_LEN
    slc_end = slc_start + SLC_LEN - 1
    ovl = ((cmp_start[:, None] <= slc_end[None, :]) & (cmp_end[:, None] >= slc_start[None, :]))
    ovl[ncmp - 1, :] = False
    ovl_t = jnp.asarray(ovl.T.astype(np.float32), MXU_DTYPE)

    kv_spec = pl.BlockSpec((1, 1, T, hd), lambda b, g, i: (b, g, 0, 0))
    row_spec = pl.BlockSpec((1, 1, ncmp, CMP_STRIDE * hd), lambda b, g, i: (b, g, 0, 0))

    def whole(a):
        return pl.BlockSpec(a.shape, lambda b, g, i: (0,) * a.ndim)

    consts = (pe_k, pe_v, w1k, w2k, w1v, w2v, e, ovl_t)
    return pl.pallas_call(
        functools.partial(_nsa_kernel, tk=tk, seq=T),
        out_shape=jax.ShapeDtypeStruct((B, T, NSA_HEADS * hd), MXU_DTYPE),
        grid=(B, NSA_KV, T // tq),
        in_specs=[pl.BlockSpec((1, NSA_REP, tq, hd), lambda b, g, i: (b, g, i, 0)),
                  row_spec, row_spec, kv_spec, kv_spec, kv_spec, kv_spec,
                  pl.BlockSpec((1, 1, tq, NSA_REP * 3), lambda b, g, i: (b, g, i, 0))]
                 + [whole(a) for a in consts],
        out_specs=pl.BlockSpec((1, tq, NSA_REP * hd), lambda b, g, i: (b, i, g)),
        scratch_shapes=[pltpu.VMEM((ncmp, hd), MXU_DTYPE), pltpu.VMEM((ncmp, hd), MXU_DTYPE)],
        compiler_params=pltpu.CompilerParams(
            dimension_semantics=("parallel", "parallel", "arbitrary"), vmem_limit_bytes=VMEM_LIMIT),
        name="nsa",
    )(qn, kc_rows, vc_rows, ksl, vsl, kw, vw, gates, *consts)


def _dsa_kernel(qd_ref, qi_ref, wi_ref, kd_ref, vd_ref, ki_ref, tri_ref, o_ref, key_ref,
                *, ck, ktop):
    tq = TQ
    s0 = pl.program_id(1) * tq
    n_kc = s0 // ck + 1
    hd = HEAD_DIM
    rows = DSA_HEADS * tq

    w = wi_ref[0]
    q_idx = qi_ref[0].reshape(IDX_HEADS * tq, IDX_DIM)
    krow = lax.broadcasted_iota(I32, (ck, tq), 0)
    t_q = s0 + lax.broadcasted_iota(I32, (ck, tq), 1)

    def score_body(c, carry):
        off = pl.multiple_of(c * ck, ck)
        lg = _dot_nt(ki_ref[0, pl.ds(off, ck), :], q_idx)
        sc = w[0:1, :] * jnp.maximum(lg[:, :tq], 0.0)
        for h in range(1, IDX_HEADS):
            sc = sc + w[h:h + 1, :] * jnp.maximum(lg[:, h * tq:(h + 1) * tq], 0.0)
        sc = jnp.where(sc == 0.0, 0.0, sc)
        bits = pltpu.bitcast(sc, I32)
        key = jnp.where(bits < 0, bits ^ np.int32(0x7FFFFFFF), bits)
        key_ref[c] = jnp.where(off + krow <= t_q, key, INT_MIN)
        return carry

    lax.fori_loop(0, n_kc, score_body, 0)

    def count_ge(cand):
        def body(c, cnt):
            hit = jnp.where(key_ref[c] >= cand, 1, 0)
            return cnt + jnp.sum(hit.reshape(ck // SUBLANES, SUBLANES, tq), axis=0)

        cnt = lax.fori_loop(0, n_kc, body, jnp.zeros((SUBLANES, tq), I32))
        return jnp.sum(cnt, axis=0, keepdims=True)

    base = jnp.where(count_ge(jnp.zeros((1, tq), I32)) >= ktop, np.int32(0), INT_MIN)

    def bit_body(it, base):
        cand = base | jnp.left_shift(np.int32(1), np.int32(30) - it)
        return jnp.where(count_ge(cand) >= ktop, cand, base)

    thr = lax.fori_loop(0, 31, bit_body, base)
    n_ge = count_ge(thr)
    need_eq = (ktop - count_ge(thr + 1)).astype(F32)
    exact_cut = jnp.max(jnp.abs(n_ge - ktop)) == 0

    q = qd_ref[0].reshape(rows, hd)

    def att_body(c, carry):
        m, l, acc, eq_seen = carry
        off = pl.multiple_of(c * ck, ck)
        key = key_ref[c]

        def cut_bias():
            return jnp.where(key >= thr, 0.0, MASK_BIAS), eq_seen

        def tie_bias():
            eq_f = jnp.where(key == thr, 1.0, 0.0)
            before = _dot(tri_ref[...], eq_f.astype(MXU_DTYPE)) + eq_seen
            take = jnp.where(key > thr, 0.0,
                             jnp.where(key == thr, jnp.where(before < need_eq, 0.0, MASK_BIAS), MASK_BIAS))
            return (jnp.where(off + krow <= t_q, take, MASK_BIAS),
                    eq_seen + jnp.sum(eq_f, axis=0, keepdims=True))

        bias_t, eq_seen = lax.cond(exact_cut, cut_bias, tie_bias)
        bias = bias_t.T
        k = kd_ref[0, pl.ds(off, ck), :]
        v = vd_ref[0, pl.ds(off, ck), :]
        s = (_dot_nt(q, k).reshape(DSA_HEADS, tq, ck) + bias[None]).reshape(rows, ck)
        m, l, acc = _softmax_step(s, m, l, acc, v)
        return m, l, acc, eq_seen

    init = (jnp.full((rows, 1), M_INIT, F32), jnp.zeros((rows, 1), F32),
            jnp.zeros((rows, hd), F32), jnp.zeros((1, tq), F32))
    _, l, acc, _ = lax.fori_loop(0, n_kc, att_body, init)
    o = acc * (1.0 / l)
    o_ref[0] = jnp.concatenate([o[h * tq:(h + 1) * tq] for h in range(DSA_HEADS)],
                               axis=-1).astype(o_ref.dtype)


def _dsa(qd, qi, wi_t, kd, vd, ki, ck):
    B, _, T, hd = qd.shape
    tq = TQ
    ktop = min(DSA_TOPK, T // 4)
    tri = jnp.asarray(np.tril(np.ones((ck, ck), np.float32), -1), MXU_DTYPE)
    kv_spec = pl.BlockSpec((1, T, hd), lambda b, i: (b, 0, 0))
    return pl.pallas_call(
        functools.partial(_dsa_kernel, ck=ck, ktop=ktop),
        out_shape=jax.ShapeDtypeStruct((B, T, DSA_HEADS * hd), MXU_DTYPE),
        grid=(B, T // tq),
        in_specs=[pl.BlockSpec((1, DSA_HEADS, tq, hd), lambda b, i: (b, 0, i, 0)),
                  pl.BlockSpec((1, IDX_HEADS, tq, IDX_DIM), lambda b, i: (b, 0, i, 0)),
                  pl.BlockSpec((1, IDX_HEADS, tq), lambda b, i: (b, 0, i)),
                  kv_spec, kv_spec, kv_spec,
                  pl.BlockSpec((ck, ck), lambda b, i: (0, 0))],
        out_specs=pl.BlockSpec((1, tq, DSA_HEADS * hd), lambda b, i: (b, i, 0)),
        scratch_shapes=[pltpu.VMEM((T // ck, ck, tq), I32)],
        compiler_params=pltpu.CompilerParams(
            dimension_semantics=("parallel", "parallel"), vmem_limit_bytes=VMEM_LIMIT),
        name="dsa",
    )(qd, qi, wi_t, kd, vd, ki, tri)


def _ffn_kernel(on_ref, od_ref, x_ref, mod_ref, gpm_ref, gpf_ref, gpo_ref,
                woa_ref, wob_ref, wup_ref, wdn_ref, out_ref, *, fc):
    o = _dot(on_ref[0], woa_ref[...]) + _dot(od_ref[0], wob_ref[...])
    x1 = x_ref[0] + mod_ref[0, 2:3, :] * _rms(o, gpm_ref[...])
    h = _rms(x1, gpf_ref[...]) * (1.0 + mod_ref[0, 4:5, :]) + mod_ref[0, 3:4, :]
    hb = h.astype(MXU_DTYPE)
    d_ff = wup_ref.shape[1]
    y = jnp.zeros(x1.shape, F32)
    for c in range(d_ff // fc):
        u = jnp.maximum(_dot(hb, wup_ref[:, c * fc:(c + 1) * fc]), 0.0)
        y = y + _dot((u * u).astype(MXU_DTYPE), wdn_ref[c * fc:(c + 1) * fc, :])
    out_ref[0] = x1 + mod_ref[0, 5:6, :] * _rms(y, gpo_ref[...])


def _ffn(o_nsa, o_dsa, x, mod, g_post_mix, g_pre_ffn, g_post_ffn, w_out, w_up, w_down, tm, fc):
    B, T, D = x.shape
    dn = o_nsa.shape[-1]

    def tok(w):
        return pl.BlockSpec((1, tm, w), lambda b, i: (b, i, 0))

    def whole(a):
        return pl.BlockSpec(a.shape, lambda b, i: (0,) * a.ndim, pipeline_mode=pl.Buffered(1))

    woa, wob = w_out[:dn], w_out[dn:]
    vec = pl.BlockSpec((1, D), lambda b, i: (0, 0))
    return pl.pallas_call(
        functools.partial(_ffn_kernel, fc=fc),
        out_shape=jax.ShapeDtypeStruct((B, T, D), F32),
        grid=(B, T // tm),
        in_specs=[tok(dn), tok(o_dsa.shape[-1]), tok(D),
                  pl.BlockSpec((1, 6, D), lambda b, i: (b, 0, 0)), vec, vec, vec,
                  whole(woa), whole(wob), whole(w_up), whole(w_down)],
        out_specs=tok(D),
        compiler_params=pltpu.CompilerParams(
            dimension_semantics=("parallel", "parallel"), vmem_limit_bytes=VMEM_LIMIT),
        name="ffn",
    )(o_nsa, o_dsa, x, mod, g_post_mix, g_pre_ffn, g_post_ffn, woa, wob, w_up, w_down)


def _permute_w_in(w):
    cols = [w[:, _SEG[n][0]:_SEG[n][1]] for n in _ROPE_ORDER + _PLAIN_ORDER]
    used = sum(c.shape[1] for c in cols)
    cols.append(jnp.zeros((w.shape[0], D_PROJ - used), w.dtype))
    return jnp.concatenate(cols, axis=1).astype(MXU_DTYPE)


def kernel(x, c, positions, w_ada, b_ada, g_pre_mix, g_post_mix, g_pre_ffn, g_post_ffn,
           w_in, cmp_pe_k, cmp_pe_v, cmp_w1_k, cmp_w2_k, cmp_w1_v, cmp_w2_v,
           w_out, w_up, w_down):
    B, T, D = x.shape
    depth = w_ada.shape[0]
    tm = min(512, T)
    tk = min(512, T)
    half = HEAD_DIM // 2
    inv = ROPE_THETA ** (-jnp.arange(half, dtype=F32) / half)
    inv_tile = jnp.tile(inv, LANES // half).reshape(1, LANES)
    pos_f = positions.astype(F32).reshape(B, T, 1)
    ncmp = T // CMP_STRIDE

    for l in range(depth):
        mod = _ada(c, w_ada[l], b_ada[l]).reshape(B, 6, D)
        (qn, kc, vc, ksl, vsl, kw, vw, gates, qd, kd, vd, qi, ki, wi) = _in_proj(
            x, mod, g_pre_mix[l].reshape(1, D), _permute_w_in(w_in[l]), pos_f, inv_tile, tm)
        rows = (B, NSA_KV, ncmp, CMP_STRIDE * HEAD_DIM)
        o_nsa = _nsa(qn, kc.reshape(rows), vc.reshape(rows), ksl, vsl, kw, vw, gates,
                     cmp_pe_k[l].reshape(1, -1), cmp_pe_v[l].reshape(1, -1),
                     cmp_w1_k[l].astype(MXU_DTYPE), cmp_w2_k[l].astype(MXU_DTYPE),
                     cmp_w1_v[l].astype(MXU_DTYPE), cmp_w2_v[l].astype(MXU_DTYPE), tk=tk)
        o_dsa = _dsa(qd, qi, jnp.swapaxes(wi, -1, -2), kd, vd, ki, ck=tk)
        x = _ffn(o_nsa, o_dsa, x, mod, g_post_mix[l].reshape(1, D), g_pre_ffn[l].reshape(1, D),
                 g_post_ffn[l].reshape(1, D), w_out[l].astype(MXU_DTYPE),
                 w_up[l].astype(MXU_DTYPE), w_down[l].astype(MXU_DTYPE), tm=tm, fc=1024)
    return x
```

```python
import functools

import numpy as np
import jax
import jax.numpy as jnp
from jax import lax
from jax.experimental import pallas as pl
from jax.experimental.pallas import tpu as pltpu

HEAD_DIM = 64
NSA_HEADS = 8
NSA_KV = 2
NSA_REP = NSA_HEADS // NSA_KV
DSA_HEADS = 8
IDX_HEADS = 4
IDX_DIM = 64
CMP_LEN = 32
CMP_STRIDE = 16
SLC_LEN = 64
SLC_TOPN = 16
WIN = 512
DSA_TOPK = 256
ROPE_THETA = 10000.0
EPS = 1e-6
NEG = -1e30
MASK_BIAS = -2e30
M_INIT = -1e30

LANES = 128
SUBLANES = 8
MXU_DTYPE = jnp.bfloat16
SOFTMAX_DTYPE = jnp.bfloat16
VMEM_LIMIT = 56 * 1024 * 1024

TQ = 2 * LANES
WORD = 32
F32 = jnp.float32
I32 = jnp.int32
INT_MIN = np.int32(-2 ** 31)

_SEG = dict(q_n=(0, 512), kc=(512, 640), vc=(640, 768), ksl=(768, 896), vsl=(896, 1024),
            kw=(1024, 1152), vw=(1152, 1280), gl=(1280, 1304), q_d=(1304, 1816),
            k_d=(1816, 1880), v_d=(1880, 1944), qi=(1944, 2200), ki=(2200, 2264), wi=(2264, 2268))
_ROPE_ORDER = ("q_n", "kc", "ksl", "kw", "q_d", "qi", "k_d", "ki")
_PLAIN_ORDER = ("vc", "vsl", "vw", "v_d", "gl", "wi")
D_PROJ = 18 * LANES


def _dot(a, b):
    return jnp.dot(a, b, preferred_element_type=F32)


def _dot_nt(a, b):
    return lax.dot_general(a, b, (((1,), (1,)), ((), ())), preferred_element_type=F32)


def _rms(x, g):
    return x * lax.rsqrt(jnp.mean(x * x, axis=-1, keepdims=True) + EPS) * g


def _ada_kernel(c_ref, w_ref, b_ref, o_ref):
    o_ref[...] = _dot(c_ref[...].astype(MXU_DTYPE), w_ref[...].astype(MXU_DTYPE)) + b_ref[...]


def _ada(c, w, b):
    B, D = c.shape
    n = w.shape[1] // D
    return pl.pallas_call(
        _ada_kernel,
        out_shape=jax.ShapeDtypeStruct((B, n * D), F32),
        grid=(n,),
        in_specs=[pl.BlockSpec((B, D), lambda j: (0, 0)),
                  pl.BlockSpec((D, D), lambda j: (0, j)),
                  pl.BlockSpec((1, D), lambda j: (0, j))],
        out_specs=pl.BlockSpec((B, D), lambda j: (0, j)),
        name="ada",
    )(c, w, b.reshape(1, -1))


def _inproj_kernel(x_ref, mod_ref, g_ref, w_ref, pos_ref, inv_ref,
                   qn_ref, kc_ref, vc_ref, ksl_ref, vsl_ref, kw_ref, vw_ref, gate_ref,
                   qd_ref, kd_ref, vd_ref, qi_ref, ki_ref, wi_ref):
    x = x_ref[0]
    tm = x.shape[0]
    h = _rms(x, g_ref[...]) * (1.0 + mod_ref[0, 1:2, :]) + mod_ref[0, 0:1, :]
    proj = _dot(h.astype(MXU_DTYPE), w_ref[...])

    ang = pos_ref[0] * inv_ref[...]
    lane = lax.broadcasted_iota(I32, (tm, LANES), 1)
    first_half = (lane & (HEAD_DIM // 2)) == 0
    cosv = jnp.cos(ang)
    sinv = jnp.sin(ang)
    sin_signed = jnp.where(first_half, -sinv, sinv)

    def rope(j):
        c = proj[:, j * LANES:(j + 1) * LANES]
        swapped = jnp.where(first_half, pltpu.roll(c, LANES - HEAD_DIM // 2, 1),
                            pltpu.roll(c, HEAD_DIM // 2, 1))
        return c * cosv + swapped * sin_signed

    def plain(j):
        return proj[:, j * LANES:(j + 1) * LANES]

    qscale = HEAD_DIM ** -0.5
    dt = qn_ref.dtype
    for j in range(4):
        r = rope(j) * qscale
        qn_ref[0, 2 * j] = r[:, :HEAD_DIM].astype(dt)
        qn_ref[0, 2 * j + 1] = r[:, HEAD_DIM:].astype(dt)
    for ref, j in ((kc_ref, 4), (ksl_ref, 5), (kw_ref, 6)):
        r = rope(j)
        ref[0, 0] = r[:, :HEAD_DIM].astype(dt)
        ref[0, 1] = r[:, HEAD_DIM:].astype(dt)
    for j in range(4):
        r = rope(7 + j) * qscale
        qd_ref[0, 2 * j] = r[:, :HEAD_DIM].astype(dt)
        qd_ref[0, 2 * j + 1] = r[:, HEAD_DIM:].astype(dt)
    for j in range(2):
        r = rope(11 + j)
        qi_ref[0, 2 * j] = r[:, :HEAD_DIM].astype(dt)
        qi_ref[0, 2 * j + 1] = r[:, HEAD_DIM:].astype(dt)
    r = rope(13)
    kd_ref[0] = r[:, :HEAD_DIM].astype(dt)
    ki_ref[0] = r[:, HEAD_DIM:].astype(dt)
    v = plain(14)
    vc_ref[0, 0] = v[:, :HEAD_DIM].astype(dt)
    vc_ref[0, 1] = v[:, HEAD_DIM:].astype(dt)
    ones_col = jnp.where(lane == HEAD_DIM, 1.0, 0.0)

    def with_ones(c):
        return jnp.where(lane < HEAD_DIM, c, ones_col).astype(dt)

    for ref, j in ((vsl_ref, 15), (vw_ref, 16)):
        v = plain(j)
        ref[0, 0] = with_ones(v)
        ref[0, 1] = with_ones(pltpu.roll(v, HEAD_DIM, 1))
    last = plain(17)
    vd_ref[0] = with_ones(last)
    ng = NSA_REP * 3
    gates = jax.nn.sigmoid(last[:, HEAD_DIM:HEAD_DIM + 2 * ng])
    gate_ref[0, 0] = gates[:, :ng]
    gate_ref[0, 1] = gates[:, ng:]
    idx_scale = (IDX_HEADS ** -0.5) * (IDX_DIM ** -0.5)
    wi_ref[0] = last[:, HEAD_DIM + 2 * ng:HEAD_DIM + 2 * ng + IDX_HEADS] * idx_scale


def _in_proj(x, mod, g_pre, w_perm, pos_f, inv_tile, tm):
    B, T, D = x.shape
    dt = MXU_DTYPE
    hd = HEAD_DIM

    def heads(n, w=hd):
        return (jax.ShapeDtypeStruct((B, n, T, w), dt),
                pl.BlockSpec((1, n, tm, w), lambda b, i: (b, 0, i, 0)))

    def flat(w, d=dt):
        return (jax.ShapeDtypeStruct((B, T, w), d),
                pl.BlockSpec((1, tm, w), lambda b, i: (b, i, 0)))

    outs = [heads(NSA_HEADS),
            heads(NSA_KV), heads(NSA_KV),
            heads(NSA_KV), heads(NSA_KV, LANES),
            heads(NSA_KV), heads(NSA_KV, LANES),
            (jax.ShapeDtypeStruct((B, NSA_KV, T, NSA_REP * 3), F32),
             pl.BlockSpec((1, NSA_KV, tm, NSA_REP * 3), lambda b, i: (b, 0, i, 0))),
            heads(DSA_HEADS),
            flat(hd), flat(LANES),
            heads(IDX_HEADS),
            flat(hd),
            flat(IDX_HEADS, F32)]
    return pl.pallas_call(
        _inproj_kernel,
        out_shape=[o[0] for o in outs],
        grid=(B, T // tm),
        in_specs=[pl.BlockSpec((1, tm, D), lambda b, i: (b, i, 0)),
                  pl.BlockSpec((1, 6, D), lambda b, i: (b, 0, 0)),
                  pl.BlockSpec((1, D), lambda b, i: (0, 0)),
                  pl.BlockSpec((D, D_PROJ), lambda b, i: (0, 0)),
                  pl.BlockSpec((1, tm, 1), lambda b, i: (b, i, 0)),
                  pl.BlockSpec((1, LANES), lambda b, i: (0, 0))],
        out_specs=[o[1] for o in outs],
        compiler_params=pltpu.CompilerParams(
            dimension_semantics=("parallel", "parallel"), vmem_limit_bytes=VMEM_LIMIT),
        name="in_proj",
    )(x, mod, g_pre, w_perm, pos_f, inv_tile)


def _softmax_probs(s, bias, m_prev):
    tq, n = bias.shape
    sb = (s.astype(SOFTMAX_DTYPE).reshape(-1, tq, n) + bias.astype(SOFTMAX_DTYPE)[None]).reshape(s.shape)
    m_new = jnp.maximum(m_prev, jnp.max(sb, axis=-1, keepdims=True).astype(F32))
    return m_new, jnp.exp(sb - m_new.astype(SOFTMAX_DTYPE))


def _softmax_step(s, bias, m_prev, acc_prev, v1):
    m_new, p = _softmax_probs(s, bias, m_prev)
    acc_new = jnp.exp(m_prev - m_new) * acc_prev + _dot(p.astype(v1.dtype), v1)
    return m_new, acc_new


def _normalize(acc):
    return acc[:, :HEAD_DIM] * (1.0 / acc[:, HEAD_DIM:HEAD_DIM + 1])


def _nsa_kernel(q_ref, kc_ref, vc_ref, ksl_ref, vsl_ref, kw_ref, vw_ref, gate_ref,
                pek_ref, pev_ref, w1k_ref, w2k_ref, w1v_ref, w2v_ref, e_ref, ovlt_ref,
                o_ref, kcmp_ref, vcmp_ref, *, tk, seq):
    tq = TQ
    qi = pl.program_id(2)
    s0 = qi * tq
    ncmp = seq // CMP_STRIDE
    nslc = seq // SLC_LEN
    nsel = min(SLC_TOPN, nslc)
    rows = NSA_REP * tq
    hd = HEAD_DIM

    @pl.when(qi == 0)
    def _():
        half = CMP_STRIDE * hd

        def compress(r_ref, pe_ref, w1_ref, w2_ref, out_ref):
            r = r_ref[0, 0]
            a = _dot(r, w1_ref[:half, :])
            b = _dot(r, w1_ref[half:, :])
            pe = jnp.broadcast_to(pe_ref[...], (SUBLANES, CMP_LEN * hd)).astype(MXU_DTYPE)
            bias = _dot(pe, w1_ref[...])[0:1, :]
            pre = a + pltpu.roll(b, ncmp - 1, 0) + bias
            out_ref[...] = _dot(jax.nn.gelu(pre).astype(MXU_DTYPE), w2_ref[...]).astype(out_ref.dtype)

        compress(kc_ref, pek_ref, w1k_ref, w2k_ref, kcmp_ref)
        compress(vc_ref, pev_ref, w1v_ref, w2v_ref, vcmp_ref)

    q = q_ref[0].reshape(rows, hd)

    s_c = _dot_nt(q, kcmp_ref[...]).reshape(NSA_REP, tq, ncmp)
    span = WIN + tq
    w0 = pl.multiple_of(jnp.maximum(s0 - WIN, 0), tq)
    kwv = kw_ref[0, 0, pl.ds(w0, span), :]
    vwv = vw_ref[0, 0, pl.ds(w0, span), :]
    s_w = _dot_nt(q, kwv)

    t_c = s0 + lax.broadcasted_iota(I32, (1, tq, ncmp), 1)
    n_c = lax.broadcasted_iota(I32, (1, tq, ncmp), 2)
    mask_c = (n_c * CMP_STRIDE + (CMP_LEN - 1)) <= t_c
    s_c = jnp.where(mask_c, s_c, NEG)
    p_c = jnp.exp(s_c - jnp.max(s_c, axis=-1, keepdims=True)) * jnp.where(mask_c, 1.0, 0.0)
    l_c = jnp.sum(p_c, axis=-1, keepdims=True)
    p_c = p_c * (1.0 / jnp.maximum(l_c, 1e-30))
    o_cmp = _dot(p_c.reshape(rows, ncmp).astype(MXU_DTYPE), vcmp_ref[...])

    p_sum = jnp.sum(p_c, axis=0)
    p_hi = p_sum.astype(MXU_DTYPE)
    p_lo = (p_sum - p_hi.astype(F32)).astype(MXU_DTYPE)
    imp = _dot_nt(ovlt_ref[...], p_hi) + _dot_nt(ovlt_ref[...], p_lo)

    kpos = w0 + lax.broadcasted_iota(I32, (tq, span), 1)
    tt = s0 + lax.broadcasted_iota(I32, (tq, span), 0)
    bias_w = jnp.where(kpos <= tt, jnp.where(kpos > tt - WIN, 0.0, MASK_BIAS), MASK_BIAS)
    _, p_w = _softmax_probs(s_w, bias_w, jnp.full((rows, 1), M_INIT, F32))
    o_win = _normalize(_dot(p_w.astype(MXU_DTYPE), vwv))

    j_idx = lax.broadcasted_iota(I32, (nslc, tq), 0)
    t_s = s0 + lax.broadcasted_iota(I32, (nslc, tq), 1)
    cur = lax.shift_right_logical(t_s, int(np.log2(SLC_LEN)))
    sc = jnp.where(j_idx == 0, jnp.inf,
                   jnp.where(j_idx == cur, jnp.inf, jnp.where(j_idx == cur - 1, jnp.inf, imp)))
    sc = jnp.where(j_idx <= cur, sc, -jnp.inf)
    rank = jnp.zeros((nslc, tq), I32)
    for i in range(nslc):
        ci = sc[i:i + 1, :]
        rank = rank + jnp.where(ci > sc, 1, jnp.where(ci == sc, jnp.where(j_idx > i, 1, 0), 0))
    blk_bias = jnp.where(rank < nsel, jnp.where(j_idx <= cur, 0.0, MASK_BIAS), MASK_BIAS)
    blk_bias = blk_bias.T.astype(MXU_DTYPE)

    def sel_body(kt, carry):
        m, acc = carry
        off = pl.multiple_of(kt * tk, tk)
        k = ksl_ref[0, 0, pl.ds(off, tk), :]
        v1 = vsl_ref[0, 0, pl.ds(off, tk), :]
        bias = _dot(blk_bias, e_ref[kt])
        kpos = off + lax.broadcasted_iota(I32, (tq, tk), 1)
        tt = s0 + lax.broadcasted_iota(I32, (tq, tk), 0)
        bias = jnp.where(kpos <= tt, bias, MASK_BIAS)
        return _softmax_step(_dot_nt(q, k), bias, m, acc, v1)

    init = (jnp.full((rows, 1), M_INIT, F32), jnp.zeros((rows, LANES), F32))
    _, acc_s = lax.fori_loop(0, s0 // tk + 1, sel_body, init)
    o_slc = _normalize(acc_s)

    g = gate_ref[0, 0]
    outs = []
    for r in range(NSA_REP):
        sl = slice(r * tq, (r + 1) * tq)
        outs.append(g[:, 3 * r:3 * r + 1] * o_cmp[sl]
                    + g[:, 3 * r + 1:3 * r + 2] * o_slc[sl]
                    + g[:, 3 * r + 2:3 * r + 3] * o_win[sl])
    o_ref[0] = jnp.concatenate(outs, axis=-1).astype(o_ref.dtype)


def _nsa(qn, kc_rows, vc_rows, ksl, vsl, kw, vw, gates, pe_k, pe_v, w1k, w2k, w1v, w2v, tk):
    B, _, T, hd = qn.shape
    tq = TQ
    ncmp = T // CMP_STRIDE
    nslc = T // SLC_LEN
    key = np.arange(T)
    e = (key[None, :] // SLC_LEN == np.arange(nslc)[:, None]).astype(np.float32)
    e = jnp.asarray(e.reshape(nslc, T // tk, tk).transpose(1, 0, 2), MXU_DTYPE)
    cmp_start = np.arange(ncmp) * CMP_STRIDE
    cmp_end = cmp_start + CMP_LEN - 1
    slc_start = np.arange(nslc) * SLC_LEN
    slc_end = slc_start + SLC_LEN - 1
    ovl = ((cmp_start[:, None] <= slc_end[None, :]) & (cmp_end[:, None] >= slc_start[None, :]))
    ovl[ncmp - 1, :] = False
    ovl_t = jnp.asarray(ovl.T.astype(np.float32), MXU_DTYPE)

    k_spec = pl.BlockSpec((1, 1, T, hd), lambda b, g, i: (b, g, 0, 0))
    v_spec = pl.BlockSpec((1, 1, T, LANES), lambda b, g, i: (b, g, 0, 0))
    row_spec = pl.BlockSpec((1, 1, ncmp, CMP_STRIDE * hd), lambda b, g, i: (b, g, 0, 0))

    def whole(a):
        return pl.BlockSpec(a.shape, lambda b, g, i: (0,) * a.ndim)

    consts = (pe_k, pe_v, w1k, w2k, w1v, w2v, e, ovl_t)
    return pl.pallas_call(
        functools.partial(_nsa_kernel, tk=tk, seq=T),
        out_shape=jax.ShapeDtypeStruct((B, T, NSA_HEADS * hd), MXU_DTYPE),
        grid=(B, NSA_KV, T // tq),
        in_specs=[pl.BlockSpec((1, NSA_REP, tq, hd), lambda b, g, i: (b, g, i, 0)),
                  row_spec, row_spec, k_spec, v_spec, k_spec, v_spec,
                  pl.BlockSpec((1, 1, tq, NSA_REP * 3), lambda b, g, i: (b, g, i, 0))]
                 + [whole(a) for a in consts],
        out_specs=pl.BlockSpec((1, tq, NSA_REP * hd), lambda b, g, i: (b, i, g)),
        scratch_shapes=[pltpu.VMEM((ncmp, hd), MXU_DTYPE), pltpu.VMEM((ncmp, hd), MXU_DTYPE)],
        compiler_params=pltpu.CompilerParams(
            dimension_semantics=("parallel", "parallel", "arbitrary"), vmem_limit_bytes=VMEM_LIMIT),
        name="nsa",
    )(qn, kc_rows, vc_rows, ksl, vsl, kw, vw, gates, *consts)


def _bit_transpose32(rows):
    a = list(rows)
    j, m = 16, 0x0000FFFF
    while j:
        for k in range(WORD):
            if k & j == 0:
                t = (a[k] ^ lax.shift_right_logical(a[k + j], np.int32(j))) & np.int32(m)
                a[k] = a[k] ^ t
                a[k + j] = a[k + j] ^ jnp.left_shift(t, np.int32(j))
        j >>= 1
        m ^= (m << j) & 0xFFFFFFFF
    return a


def _dsa_kernel(qd_ref, qi_ref, wi_ref, kd_ref, vd_ref, ki_ref, tri_ref, o_ref, key_ref, plane_ref,
                *, ck, ktop):
    tq = TQ
    s0 = pl.program_id(1) * tq
    n_kc = s0 // ck + 1
    hd = HEAD_DIM
    rows = DSA_HEADS * tq
    gpc = ck // (WORD * SUBLANES)
    n_groups = plane_ref.shape[1]

    w = wi_ref[0]
    q_idx = qi_ref[0].reshape(IDX_HEADS * tq, IDX_DIM)
    krow = lax.broadcasted_iota(I32, (ck, tq), 0)
    t_q = s0 + lax.broadcasted_iota(I32, (ck, tq), 1)

    def score_body(c, carry):
        off = pl.multiple_of(c * ck, ck)
        lg = _dot_nt(ki_ref[0, pl.ds(off, ck), :], q_idx)
        sc = w[0:1, :] * jnp.maximum(lg[:, :tq], 0.0)
        for h in range(1, IDX_HEADS):
            sc = sc + w[h:h + 1, :] * jnp.maximum(lg[:, h * tq:(h + 1) * tq], 0.0)
        sc = jnp.where(sc == 0.0, 0.0, sc)
        bits = pltpu.bitcast(sc, I32)
        key = jnp.where(bits < 0, bits ^ np.int32(0x7FFFFFFF), bits)
        key = jnp.where(off + krow <= t_q, key, INT_MIN)
        key_ref[c] = key
        for g in range(gpc):
            slabs = [key[(g * WORD + j) * SUBLANES:(g * WORD + j + 1) * SUBLANES, :] ^ INT_MIN
                     for j in range(WORD)]
            for i, plane in enumerate(_bit_transpose32(slabs)):
                plane_ref[i, c * gpc + g] = plane
        return carry

    lax.fori_loop(0, n_kc, score_body, 0)

    def clear_body(c, carry):
        plane_ref[:, pl.ds(c * gpc, gpc)] = jnp.zeros((WORD, gpc, SUBLANES, tq), I32)
        return carry

    lax.fori_loop(n_kc, n_groups // gpc, clear_body, 0)

    def bit_body(i, carry):
        cand, need, thr_u = carry
        ones = cand & plane_ref[i]
        cnt = jnp.sum(jnp.sum(lax.population_count(ones), axis=0), axis=0, keepdims=True)
        take = cnt >= need
        cand = jnp.where(take, ones, cand ^ ones)
        need = jnp.where(take, need, need - cnt)
        thr_u = thr_u | jnp.where(take, jnp.left_shift(np.int32(1), np.int32(WORD - 1) - i), 0)
        return cand, need, thr_u

    cand, need, thr_u = lax.fori_loop(
        0, WORD, bit_body,
        (jnp.full((n_groups, SUBLANES, tq), -1, I32), jnp.full((1, tq), ktop, I32),
         jnp.zeros((1, tq), I32)))
    thr = thr_u ^ INT_MIN
    n_eq = jnp.sum(jnp.sum(lax.population_count(cand), axis=0), axis=0, keepdims=True)
    need_eq = need.astype(F32)
    exact_cut = jnp.max(jnp.where(thr_u == 0, 1, jnp.abs(n_eq - need))) == 0

    q = qd_ref[0].reshape(rows, hd)

    def att_body(c, carry):
        m, acc, eq_seen = carry
        off = pl.multiple_of(c * ck, ck)
        key = key_ref[c]

        def cut_bias():
            return jnp.where(key >= thr, 0.0, MASK_BIAS), eq_seen

        def tie_bias():
            eq_f = jnp.where(key == thr, 1.0, 0.0)
            before = _dot(tri_ref[...], eq_f.astype(MXU_DTYPE)) + eq_seen
            take = jnp.where(key > thr, 0.0,
                             jnp.where(key == thr, jnp.where(before < need_eq, 0.0, MASK_BIAS), MASK_BIAS))
            return (jnp.where(off + krow <= t_q, take, MASK_BIAS),
                    eq_seen + jnp.sum(eq_f, axis=0, keepdims=True))

        bias_t, eq_seen = lax.cond(exact_cut, cut_bias, tie_bias)
        bias = bias_t.T
        k = kd_ref[0, pl.ds(off, ck), :]
        v1 = vd_ref[0, pl.ds(off, ck), :]
        m, acc = _softmax_step(_dot_nt(q, k), bias, m, acc, v1)
        return m, acc, eq_seen

    init = (jnp.full((rows, 1), M_INIT, F32), jnp.zeros((rows, LANES), F32),
            jnp.zeros((1, tq), F32))
    _, acc, _ = lax.fori_loop(0, n_kc, att_body, init)
    o = _normalize(acc)
    o_ref[0] = jnp.concatenate([o[h * tq:(h + 1) * tq] for h in range(DSA_HEADS)],
                               axis=-1).astype(o_ref.dtype)


def _dsa(qd, qi, wi_t, kd, vd, ki, ck):
    B, _, T, hd = qd.shape
    tq = TQ
    ktop = min(DSA_TOPK, T // 4)
    tri = jnp.asarray(np.tril(np.ones((ck, ck), np.float32), -1), MXU_DTYPE)
    kv_spec = pl.BlockSpec((1, T, hd), lambda b, i: (b, 0, 0))
    v_spec = pl.BlockSpec((1, T, LANES), lambda b, i: (b, 0, 0))
    return pl.pallas_call(
        functools.partial(_dsa_kernel, ck=ck, ktop=ktop),
        out_shape=jax.ShapeDtypeStruct((B, T, DSA_HEADS * hd), MXU_DTYPE),
        grid=(B, T // tq),
        in_specs=[pl.BlockSpec((1, DSA_HEADS, tq, hd), lambda b, i: (b, 0, i, 0)),
                  pl.BlockSpec((1, IDX_HEADS, tq, IDX_DIM), lambda b, i: (b, 0, i, 0)),
                  pl.BlockSpec((1, IDX_HEADS, tq), lambda b, i: (b, 0, i)),
                  kv_spec, v_spec, kv_spec,
                  pl.BlockSpec((ck, ck), lambda b, i: (0, 0))],
        out_specs=pl.BlockSpec((1, tq, DSA_HEADS * hd), lambda b, i: (b, i, 0)),
        scratch_shapes=[pltpu.VMEM((T // ck, ck, tq), I32),
                        pltpu.VMEM((WORD, T // (WORD * SUBLANES), SUBLANES, tq), I32)],
        compiler_params=pltpu.CompilerParams(
            dimension_semantics=("parallel", "parallel"), vmem_limit_bytes=VMEM_LIMIT),
        name="dsa",
    )(qd, qi, wi_t, kd, vd, ki, tri)


def _ffn_kernel(on_ref, od_ref, x_ref, mod_ref, gpm_ref, gpf_ref, gpo_ref,
                woa_ref, wob_ref, wup_ref, wdn_ref, out_ref, *, fc):
    o = _dot(on_ref[0], woa_ref[...]) + _dot(od_ref[0], wob_ref[...])
    x1 = x_ref[0] + mod_ref[0, 2:3, :] * _rms(o, gpm_ref[...])
    h = _rms(x1, gpf_ref[...]) * (1.0 + mod_ref[0, 4:5, :]) + mod_ref[0, 3:4, :]
    hb = h.astype(MXU_DTYPE)
    d_ff = wup_ref.shape[1]
    y = jnp.zeros(x1.shape, F32)
    for c in range(d_ff // fc):
        u = jnp.maximum(_dot(hb, wup_ref[:, c * fc:(c + 1) * fc]), 0.0)
        y = y + _dot((u * u).astype(MXU_DTYPE), wdn_ref[c * fc:(c + 1) * fc, :])
    out_ref[0] = x1 + mod_ref[0, 5:6, :] * _rms(y, gpo_ref[...])


def _ffn(o_nsa, o_dsa, x, mod, g_post_mix, g_pre_ffn, g_post_ffn, w_out, w_up, w_down, tm, fc):
    B, T, D = x.shape
    dn = o_nsa.shape[-1]

    def tok(w):
        return pl.BlockSpec((1, tm, w), lambda b, i: (b, i, 0))

    def whole(a):
        return pl.BlockSpec(a.shape, lambda b, i: (0,) * a.ndim, pipeline_mode=pl.Buffered(1))

    woa, wob = w_out[:dn], w_out[dn:]
    vec = pl.BlockSpec((1, D), lambda b, i: (0, 0))
    return pl.pallas_call(
        functools.partial(_ffn_kernel, fc=fc),
        out_shape=jax.ShapeDtypeStruct((B, T, D), F32),
        grid=(B, T // tm),
        in_specs=[tok(dn), tok(o_dsa.shape[-1]), tok(D),
                  pl.BlockSpec((1, 6, D), lambda b, i: (b, 0, 0)), vec, vec, vec,
                  whole(woa), whole(wob), whole(w_up), whole(w_down)],
        out_specs=tok(D),
        compiler_params=pltpu.CompilerParams(
            dimension_semantics=("parallel", "parallel"), vmem_limit_bytes=VMEM_LIMIT),
        name="ffn",
    )(o_nsa, o_dsa, x, mod, g_post_mix, g_pre_ffn, g_post_ffn, woa, wob, w_up, w_down)


def _permute_w_in(w):
    cols = [w[:, _SEG[n][0]:_SEG[n][1]] for n in _ROPE_ORDER + _PLAIN_ORDER]
    used = sum(c.shape[1] for c in cols)
    cols.append(jnp.zeros((w.shape[0], D_PROJ - used), w.dtype))
    return jnp.concatenate(cols, axis=1).astype(MXU_DTYPE)


def kernel(x, c, positions, w_ada, b_ada, g_pre_mix, g_post_mix, g_pre_ffn, g_post_ffn,
           w_in, cmp_pe_k, cmp_pe_v, cmp_w1_k, cmp_w2_k, cmp_w1_v, cmp_w2_v,
           w_out, w_up, w_down):
    B, T, D = x.shape
    depth = w_ada.shape[0]
    tm = min(512, T)
    tk = min(512, T)
    half = HEAD_DIM // 2
    inv = ROPE_THETA ** (-jnp.arange(half, dtype=F32) / half)
    inv_tile = jnp.tile(inv, LANES // half).reshape(1, LANES)
    pos_f = positions.astype(F32).reshape(B, T, 1)
    ncmp = T // CMP_STRIDE

    for l in range(depth):
        mod = _ada(c, w_ada[l], b_ada[l]).reshape(B, 6, D)
        (qn, kc, vc, ksl, vsl, kw, vw, gates, qd, kd, vd, qi, ki, wi) = _in_proj(
            x, mod, g_pre_mix[l].reshape(1, D), _permute_w_in(w_in[l]), pos_f, inv_tile, tm)
        rows = (B, NSA_KV, ncmp, CMP_STRIDE * HEAD_DIM)
        o_nsa = _nsa(qn, kc.reshape(rows), vc.reshape(rows), ksl, vsl, kw, vw, gates,
                     cmp_pe_k[l].reshape(1, -1), cmp_pe_v[l].reshape(1, -1),
                     cmp_w1_k[l].astype(MXU_DTYPE), cmp_w2_k[l].astype(MXU_DTYPE),
                     cmp_w1_v[l].astype(MXU_DTYPE), cmp_w2_v[l].astype(MXU_DTYPE), tk=tk)
        o_dsa = _dsa(qd, qi, jnp.swapaxes(wi, -1, -2), kd, vd, ki, ck=tk)
        x = _ffn(o_nsa, o_dsa, x, mod, g_post_mix[l].reshape(1, D), g_pre_ffn[l].reshape(1, D),
                 g_post_ffn[l].reshape(1, D), w_out[l].astype(MXU_DTYPE),
                 w_up[l].astype(MXU_DTYPE), w_down[l].astype(MXU_DTYPE), tm=tm, fc=1024)
    return x
```

```python
import functools

import numpy as np
import jax
import jax.numpy as jnp
from jax import lax
from jax.experimental import pallas as pl
from jax.experimental.pallas import tpu as pltpu

HEAD_DIM = 64
NSA_HEADS = 8
NSA_KV = 2
NSA_REP = NSA_HEADS // NSA_KV
DSA_HEADS = 8
IDX_HEADS = 4
IDX_DIM = 64
CMP_LEN = 32
CMP_STRIDE = 16
SLC_LEN = 64
SLC_TOPN = 16
WIN = 512
DSA_TOPK = 256
ROPE_THETA = 10000.0
EPS = 1e-6
NEG = -1e30
MASK_BIAS = -2e30
M_INIT = -1e30

LANES = 128
SUBLANES = 8
MXU_DTYPE = jnp.bfloat16
SOFTMAX_DTYPE = jnp.bfloat16
VMEM_LIMIT = 56 * 1024 * 1024

TQ = 2 * LANES
WORD = 32
F32 = jnp.float32
I32 = jnp.int32
INT_MIN = np.int32(-2 ** 31)

_SEG = dict(q_n=(0, 512), kc=(512, 640), vc=(640, 768), ksl=(768, 896), vsl=(896, 1024),
            kw=(1024, 1152), vw=(1152, 1280), gl=(1280, 1304), q_d=(1304, 1816),
            k_d=(1816, 1880), v_d=(1880, 1944), qi=(1944, 2200), ki=(2200, 2264), wi=(2264, 2268))
_ROPE_ORDER = ("q_n", "kc", "ksl", "kw", "q_d", "qi", "k_d", "ki")
_PLAIN_ORDER = ("vc", "vsl", "vw", "v_d", "gl", "wi")
D_PROJ = 18 * LANES


def _dot(a, b):
    return jnp.dot(a, b, preferred_element_type=F32)


def _dot_nt(a, b):
    return lax.dot_general(a, b, (((1,), (1,)), ((), ())), preferred_element_type=F32)


def _rms(x, g):
    return x * lax.rsqrt(jnp.mean(x * x, axis=-1, keepdims=True) + EPS) * g


def _ada_kernel(c_ref, w_ref, b_ref, o_ref):
    o_ref[...] = _dot(c_ref[...].astype(MXU_DTYPE), w_ref[...].astype(MXU_DTYPE)) + b_ref[...]


def _ada(c, w, b):
    B, D = c.shape
    n = w.shape[1] // D
    return pl.pallas_call(
        _ada_kernel,
        out_shape=jax.ShapeDtypeStruct((B, n * D), F32),
        grid=(n,),
        in_specs=[pl.BlockSpec((B, D), lambda j: (0, 0)),
                  pl.BlockSpec((D, D), lambda j: (0, j)),
                  pl.BlockSpec((1, D), lambda j: (0, j))],
        out_specs=pl.BlockSpec((B, D), lambda j: (0, j)),
        name="ada",
    )(c, w, b.reshape(1, -1))


def _inproj_kernel(x_ref, mod_ref, g_ref, w_ref, pos_ref, inv_ref,
                   qn_ref, kc_ref, vc_ref, ksl_ref, vsl_ref, kw_ref, vw_ref, gate_ref,
                   qd_ref, kd_ref, vd_ref, qi_ref, ki_ref, wi_ref):
    x = x_ref[0]
    tm = x.shape[0]
    h = _rms(x, g_ref[...]) * (1.0 + mod_ref[0, 1:2, :]) + mod_ref[0, 0:1, :]
    proj = _dot(h.astype(MXU_DTYPE), w_ref[...])

    ang = pos_ref[0] * inv_ref[...]
    lane = lax.broadcasted_iota(I32, (tm, LANES), 1)
    first_half = (lane & (HEAD_DIM // 2)) == 0
    cosv = jnp.cos(ang)
    sinv = jnp.sin(ang)
    sin_signed = jnp.where(first_half, -sinv, sinv)

    def rope(j):
        c = proj[:, j * LANES:(j + 1) * LANES]
        swapped = jnp.where(first_half, pltpu.roll(c, LANES - HEAD_DIM // 2, 1),
                            pltpu.roll(c, HEAD_DIM // 2, 1))
        return c * cosv + swapped * sin_signed

    def plain(j):
        return proj[:, j * LANES:(j + 1) * LANES]

    qscale = HEAD_DIM ** -0.5
    dt = qn_ref.dtype
    for j in range(4):
        r = rope(j) * qscale
        qn_ref[0, 2 * j] = r[:, :HEAD_DIM].astype(dt)
        qn_ref[0, 2 * j + 1] = r[:, HEAD_DIM:].astype(dt)
    for ref, j in ((kc_ref, 4), (ksl_ref, 5), (kw_ref, 6)):
        r = rope(j)
        ref[0, 0] = r[:, :HEAD_DIM].astype(dt)
        ref[0, 1] = r[:, HEAD_DIM:].astype(dt)
    for j in range(4):
        r = rope(7 + j) * qscale
        qd_ref[0, 2 * j] = r[:, :HEAD_DIM].astype(dt)
        qd_ref[0, 2 * j + 1] = r[:, HEAD_DIM:].astype(dt)
    for j in range(2):
        r = rope(11 + j)
        qi_ref[0, 2 * j] = r[:, :HEAD_DIM].astype(dt)
        qi_ref[0, 2 * j + 1] = r[:, HEAD_DIM:].astype(dt)
    r = rope(13)
    kd_ref[0] = r[:, :HEAD_DIM].astype(dt)
    ki_ref[0] = r[:, HEAD_DIM:].astype(dt)
    v = plain(14)
    vc_ref[0, 0] = v[:, :HEAD_DIM].astype(dt)
    vc_ref[0, 1] = v[:, HEAD_DIM:].astype(dt)
    ones_col = jnp.where(lane == HEAD_DIM, 1.0, 0.0)

    def with_ones(c):
        return jnp.where(lane < HEAD_DIM, c, ones_col).astype(dt)

    for ref, j in ((vsl_ref, 15), (vw_ref, 16)):
        v = plain(j)
        ref[0, 0] = with_ones(v)
        ref[0, 1] = with_ones(pltpu.roll(v, HEAD_DIM, 1))
    last = plain(17)
    vd_ref[0] = with_ones(last)
    ng = NSA_REP * 3
    gates = jax.nn.sigmoid(last[:, HEAD_DIM:HEAD_DIM + 2 * ng])
    gate_ref[0, 0] = gates[:, :ng]
    gate_ref[0, 1] = gates[:, ng:]
    idx_scale = (IDX_HEADS ** -0.5) * (IDX_DIM ** -0.5)
    wi_ref[0] = last[:, HEAD_DIM + 2 * ng:HEAD_DIM + 2 * ng + IDX_HEADS] * idx_scale


def _in_proj(x, mod, g_pre, w_perm, pos_f, inv_tile, tm):
    B, T, D = x.shape
    dt = MXU_DTYPE
    hd = HEAD_DIM

    def heads(n, w=hd):
        return (jax.ShapeDtypeStruct((B, n, T, w), dt),
                pl.BlockSpec((1, n, tm, w), lambda b, i: (b, 0, i, 0)))

    def flat(w, d=dt):
        return (jax.ShapeDtypeStruct((B, T, w), d),
                pl.BlockSpec((1, tm, w), lambda b, i: (b, i, 0)))

    outs = [heads(NSA_HEADS),
            heads(NSA_KV), heads(NSA_KV),
            heads(NSA_KV), heads(NSA_KV, LANES),
            heads(NSA_KV), heads(NSA_KV, LANES),
            (jax.ShapeDtypeStruct((B, NSA_KV, T, NSA_REP * 3), F32),
             pl.BlockSpec((1, NSA_KV, tm, NSA_REP * 3), lambda b, i: (b, 0, i, 0))),
            heads(DSA_HEADS),
            flat(hd), flat(LANES),
            heads(IDX_HEADS),
            flat(hd),
            flat(IDX_HEADS, F32)]
    return pl.pallas_call(
        _inproj_kernel,
        out_shape=[o[0] for o in outs],
        grid=(B, T // tm),
        in_specs=[pl.BlockSpec((1, tm, D), lambda b, i: (b, i, 0)),
                  pl.BlockSpec((1, 6, D), lambda b, i: (b, 0, 0)),
                  pl.BlockSpec((1, D), lambda b, i: (0, 0)),
                  pl.BlockSpec((D, D_PROJ), lambda b, i: (0, 0)),
                  pl.BlockSpec((1, tm, 1), lambda b, i: (b, i, 0)),
                  pl.BlockSpec((1, LANES), lambda b, i: (0, 0))],
        out_specs=[o[1] for o in outs],
        compiler_params=pltpu.CompilerParams(
            dimension_semantics=("parallel", "parallel"), vmem_limit_bytes=VMEM_LIMIT),
        name="in_proj",
    )(x, mod, g_pre, w_perm, pos_f, inv_tile)


def _softmax_probs(s, bias, m_prev):
    tq, n = bias.shape
    sb = (s.astype(SOFTMAX_DTYPE).reshape(-1, tq, n) + bias.astype(SOFTMAX_DTYPE)[None]).reshape(s.shape)
    m_new = jnp.maximum(m_prev, jnp.max(sb, axis=-1, keepdims=True).astype(F32))
    return m_new, jnp.exp(sb - m_new.astype(SOFTMAX_DTYPE))


def _softmax_step(s, bias, m_prev, acc_prev, v1):
    m_new, p = _softmax_probs(s, bias, m_prev)
    acc_new = jnp.exp(m_prev - m_new) * acc_prev + _dot(p.astype(v1.dtype), v1)
    return m_new, acc_new


def _normalize(acc):
    return acc[:, :HEAD_DIM] * (1.0 / acc[:, HEAD_DIM:HEAD_DIM + 1])


def _nsa_kernel(q_ref, kc_ref, vc_ref, ksl_ref, vsl_ref, kw_ref, vw_ref, gate_ref,
                pek_ref, pev_ref, w1k_ref, w2k_ref, w1v_ref, w2v_ref, e_ref, ovlt_ref,
                o_ref, kcmp_ref, vcmp_ref, *, tk, seq):
    tq = TQ
    qi = pl.program_id(1)
    s0 = qi * tq
    ncmp = seq // CMP_STRIDE
    nslc = seq // SLC_LEN
    nsel = min(SLC_TOPN, nslc)
    rows = NSA_REP * tq
    hd = HEAD_DIM
    groups = range(NSA_KV)

    @pl.when(qi == 0)
    def _():
        half = CMP_STRIDE * hd

        def compress(r, pe_ref, w1_ref, w2_ref):
            a = _dot(r, w1_ref[:half, :])
            b = _dot(r, w1_ref[half:, :])
            pe = jnp.broadcast_to(pe_ref[...], (SUBLANES, CMP_LEN * hd)).astype(MXU_DTYPE)
            bias = _dot(pe, w1_ref[...])[0:1, :]
            pre = a + pltpu.roll(b, ncmp - 1, 0) + bias
            return _dot(jax.nn.gelu(pre).astype(MXU_DTYPE), w2_ref[...])

        for g in groups:
            kcmp_ref[g] = compress(kc_ref[0, g], pek_ref, w1k_ref, w2k_ref).astype(kcmp_ref.dtype)
            vcmp_ref[g] = compress(vc_ref[0, g], pev_ref, w1v_ref, w2v_ref).astype(vcmp_ref.dtype)

    q = [q_ref[0, g * NSA_REP:(g + 1) * NSA_REP].reshape(rows, hd) for g in groups]

    span = WIN + tq
    w0 = pl.multiple_of(jnp.maximum(s0 - WIN, 0), tq)
    s_c = [_dot_nt(q[g], kcmp_ref[g]).reshape(NSA_REP, tq, ncmp) for g in groups]
    s_w = [_dot_nt(q[g], kw_ref[0, g, pl.ds(w0, span), :]) for g in groups]

    t_c = s0 + lax.broadcasted_iota(I32, (1, tq, ncmp), 1)
    n_c = lax.broadcasted_iota(I32, (1, tq, ncmp), 2)
    mask_c = (n_c * CMP_STRIDE + (CMP_LEN - 1)) <= t_c
    maskf_c = jnp.where(mask_c, 1.0, 0.0)
    o_cmp, imp = [], []
    for g in groups:
        s_m = jnp.where(mask_c, s_c[g], NEG)
        p_c = jnp.exp(s_m - jnp.max(s_m, axis=-1, keepdims=True)) * maskf_c
        l_c = jnp.sum(p_c, axis=-1, keepdims=True)
        p_c = p_c * (1.0 / jnp.maximum(l_c, 1e-30))
        o_cmp.append(_dot(p_c.reshape(rows, ncmp).astype(MXU_DTYPE), vcmp_ref[g]))
        p_sum = jnp.sum(p_c, axis=0)
        p_hi = p_sum.astype(MXU_DTYPE)
        p_lo = (p_sum - p_hi.astype(F32)).astype(MXU_DTYPE)
        imp.append(_dot_nt(ovlt_ref[...], p_hi) + _dot_nt(ovlt_ref[...], p_lo))

    kpos = w0 + lax.broadcasted_iota(I32, (tq, span), 1)
    tt = s0 + lax.broadcasted_iota(I32, (tq, span), 0)
    bias_w = jnp.where(kpos <= tt, jnp.where(kpos > tt - WIN, 0.0, MASK_BIAS), MASK_BIAS)
    o_win = []
    for g in groups:
        _, p_w = _softmax_probs(s_w[g], bias_w, jnp.full((rows, 1), M_INIT, F32))
        o_win.append(_normalize(_dot(p_w.astype(MXU_DTYPE), vw_ref[0, g, pl.ds(w0, span), :])))

    j_idx = lax.broadcasted_iota(I32, (nslc, tq), 0)
    t_s = s0 + lax.broadcasted_iota(I32, (nslc, tq), 1)
    cur = lax.shift_right_logical(t_s, int(np.log2(SLC_LEN)))
    blk_bias = []
    for g in groups:
        sc = jnp.where(j_idx == 0, jnp.inf,
                       jnp.where(j_idx == cur, jnp.inf, jnp.where(j_idx == cur - 1, jnp.inf, imp[g])))
        sc = jnp.where(j_idx <= cur, sc, -jnp.inf)
        rank = jnp.zeros((nslc, tq), I32)
        for i in range(nslc):
            ci = sc[i:i + 1, :]
            rank = rank + jnp.where(ci > sc, 1, jnp.where(ci == sc, jnp.where(j_idx > i, 1, 0), 0))
        bb = jnp.where(rank < nsel, jnp.where(j_idx <= cur, 0.0, MASK_BIAS), MASK_BIAS)
        blk_bias.append(bb.T.astype(MXU_DTYPE))

    def sel_body(kt, carry):
        off = pl.multiple_of(kt * tk, tk)
        kpos = off + lax.broadcasted_iota(I32, (tq, tk), 1)
        tt = s0 + lax.broadcasted_iota(I32, (tq, tk), 0)
        out = []
        for g in groups:
            m, acc = carry[g]
            k = ksl_ref[0, g, pl.ds(off, tk), :]
            v1 = vsl_ref[0, g, pl.ds(off, tk), :]
            bias = jnp.where(kpos <= tt, _dot(blk_bias[g], e_ref[kt]), MASK_BIAS)
            out.append(_softmax_step(_dot_nt(q[g], k), bias, m, acc, v1))
        return tuple(out)

    init = tuple((jnp.full((rows, 1), M_INIT, F32), jnp.zeros((rows, LANES), F32)) for _ in groups)
    sel = lax.fori_loop(0, s0 // tk + 1, sel_body, init)

    outs = []
    for g in groups:
        o_slc = _normalize(sel[g][1])
        gt = gate_ref[0, g]
        for r in range(NSA_REP):
            sl = slice(r * tq, (r + 1) * tq)
            outs.append(gt[:, 3 * r:3 * r + 1] * o_cmp[g][sl]
                        + gt[:, 3 * r + 1:3 * r + 2] * o_slc[sl]
                        + gt[:, 3 * r + 2:3 * r + 3] * o_win[g][sl])
    o_ref[0] = jnp.concatenate(outs, axis=-1).astype(o_ref.dtype)


def _nsa(qn, kc_rows, vc_rows, ksl, vsl, kw, vw, gates, pe_k, pe_v, w1k, w2k, w1v, w2v, tk):
    B, _, T, hd = qn.shape
    tq = TQ
    ncmp = T // CMP_STRIDE
    nslc = T // SLC_LEN
    key = np.arange(T)
    e = (key[None, :] // SLC_LEN == np.arange(nslc)[:, None]).astype(np.float32)
    e = jnp.asarray(e.reshape(nslc, T // tk, tk).transpose(1, 0, 2), MXU_DTYPE)
    cmp_start = np.arange(ncmp) * CMP_STRIDE
    cmp_end = cmp_start + CMP_LEN - 1
    slc_start = np.arange(nslc) * SLC_LEN
    slc_end = slc_start + SLC_LEN - 1
    ovl = ((cmp_start[:, None] <= slc_end[None, :]) & (cmp_end[:, None] >= slc_start[None, :]))
    ovl[ncmp - 1, :] = False
    ovl_t = jnp.asarray(ovl.T.astype(np.float32), MXU_DTYPE)

    def both(shape):
        return pl.BlockSpec((1, NSA_KV) + shape, lambda b, i: (b, 0) + (0,) * len(shape))

    def whole(a):
        return pl.BlockSpec(a.shape, lambda b, i: (0,) * a.ndim)

    consts = (pe_k, pe_v, w1k, w2k, w1v, w2v, e, ovl_t)
    return pl.pallas_call(
        functools.partial(_nsa_kernel, tk=tk, seq=T),
        out_shape=jax.ShapeDtypeStruct((B, T, NSA_HEADS * hd), MXU_DTYPE),
        grid=(B, T // tq),
        in_specs=[pl.BlockSpec((1, NSA_HEADS, tq, hd), lambda b, i: (b, 0, i, 0)),
                  both((ncmp, CMP_STRIDE * hd)), both((ncmp, CMP_STRIDE * hd)),
                  both((T, hd)), both((T, LANES)), both((T, hd)), both((T, LANES)),
                  pl.BlockSpec((1, NSA_KV, tq, NSA_REP * 3), lambda b, i: (b, 0, i, 0))]
                 + [whole(a) for a in consts],
        out_specs=pl.BlockSpec((1, tq, NSA_HEADS * hd), lambda b, i: (b, i, 0)),
        scratch_shapes=[pltpu.VMEM((NSA_KV, ncmp, hd), MXU_DTYPE),
                        pltpu.VMEM((NSA_KV, ncmp, hd), MXU_DTYPE)],
        compiler_params=pltpu.CompilerParams(
            dimension_semantics=("parallel", "arbitrary"), vmem_limit_bytes=VMEM_LIMIT),
        name="nsa",
    )(qn, kc_rows, vc_rows, ksl, vsl, kw, vw, gates, *consts)


def _bit_transpose32(rows):
    a = list(rows)
    j, m = 16, 0x0000FFFF
    while j:
        for k in range(WORD):
            if k & j == 0:
                t = (a[k] ^ lax.shift_right_logical(a[k + j], np.int32(j))) & np.int32(m)
                a[k] = a[k] ^ t
                a[k + j] = a[k + j] ^ jnp.left_shift(t, np.int32(j))
        j >>= 1
        m ^= (m << j) & 0xFFFFFFFF
    return a


def _dsa_kernel(qd_ref, qi_ref, wi_ref, kd_ref, vd_ref, ki_ref, tri_ref, o_ref, key_ref, plane_ref,
                *, ck, ktop):
    tq = TQ
    s0 = pl.program_id(1) * tq
    n_kc = s0 // ck + 1
    hd = HEAD_DIM
    rows = DSA_HEADS * tq
    gpc = ck // (WORD * SUBLANES)
    n_groups = plane_ref.shape[1]

    w = wi_ref[0]
    q_idx = qi_ref[0].reshape(IDX_HEADS * tq, IDX_DIM)
    krow = lax.broadcasted_iota(I32, (ck, tq), 0)
    t_q = s0 + lax.broadcasted_iota(I32, (ck, tq), 1)

    def score_body(c, carry):
        off = pl.multiple_of(c * ck, ck)
        lg = _dot_nt(ki_ref[0, pl.ds(off, ck), :], q_idx)
        sc = w[0:1, :] * jnp.maximum(lg[:, :tq], 0.0)
        for h in range(1, IDX_HEADS):
            sc = sc + w[h:h + 1, :] * jnp.maximum(lg[:, h * tq:(h + 1) * tq], 0.0)
        sc = jnp.where(sc == 0.0, 0.0, sc)
        bits = pltpu.bitcast(sc, I32)
        key = jnp.where(bits < 0, bits ^ np.int32(0x7FFFFFFF), bits)
        key = jnp.where(off + krow <= t_q, key, INT_MIN)
        key_ref[c] = key
        for g in range(gpc):
            slabs = [key[(g * WORD + j) * SUBLANES:(g * WORD + j + 1) * SUBLANES, :] ^ INT_MIN
                     for j in range(WORD)]
            for i, plane in enumerate(_bit_transpose32(slabs)):
                plane_ref[i, c * gpc + g] = plane
        return carry

    lax.fori_loop(0, n_kc, score_body, 0)

    def clear_body(c, carry):
        plane_ref[:, pl.ds(c * gpc, gpc)] = jnp.zeros((WORD, gpc, SUBLANES, tq), I32)
        return carry

    lax.fori_loop(n_kc, n_groups // gpc, clear_body, 0)

    def bit_body(i, carry):
        cand, need, thr_u = carry
        ones = cand & plane_ref[i]
        cnt = jnp.sum(jnp.sum(lax.population_count(ones), axis=0), axis=0, keepdims=True)
        take = cnt >= need
        cand = jnp.where(take, ones, cand ^ ones)
        need = jnp.where(take, need, need - cnt)
        thr_u = thr_u | jnp.where(take, jnp.left_shift(np.int32(1), np.int32(WORD - 1) - i), 0)
        return cand, need, thr_u

    cand, need, thr_u = lax.fori_loop(
        0, WORD, bit_body,
        (jnp.full((n_groups, SUBLANES, tq), -1, I32), jnp.full((1, tq), ktop, I32),
         jnp.zeros((1, tq), I32)))
    thr = thr_u ^ INT_MIN
    n_eq = jnp.sum(jnp.sum(lax.population_count(cand), axis=0), axis=0, keepdims=True)
    need_eq = need.astype(F32)
    exact_cut = jnp.max(jnp.where(thr_u == 0, 1, jnp.abs(n_eq - need))) == 0

    q = qd_ref[0].reshape(rows, hd)

    def att_body(c, carry):
        m, acc, eq_seen = carry
        off = pl.multiple_of(c * ck, ck)
        key = key_ref[c]

        def cut_bias():
            return jnp.where(key >= thr, 0.0, MASK_BIAS), eq_seen

        def tie_bias():
            eq_f = jnp.where(key == thr, 1.0, 0.0)
            before = _dot(tri_ref[...], eq_f.astype(MXU_DTYPE)) + eq_seen
            take = jnp.where(key > thr, 0.0,
                             jnp.where(key == thr, jnp.where(before < need_eq, 0.0, MASK_BIAS), MASK_BIAS))
            return (jnp.where(off + krow <= t_q, take, MASK_BIAS),
                    eq_seen + jnp.sum(eq_f, axis=0, keepdims=True))

        bias_t, eq_seen = lax.cond(exact_cut, cut_bias, tie_bias)
        bias = bias_t.T
        k = kd_ref[0, pl.ds(off, ck), :]
        v1 = vd_ref[0, pl.ds(off, ck), :]
        m, acc = _softmax_step(_dot_nt(q, k), bias, m, acc, v1)
        return m, acc, eq_seen

    init = (jnp.full((rows, 1), M_INIT, F32), jnp.zeros((rows, LANES), F32),
            jnp.zeros((1, tq), F32))
    _, acc, _ = lax.fori_loop(0, n_kc, att_body, init)
    o = _normalize(acc)
    o_ref[0] = jnp.concatenate([o[h * tq:(h + 1) * tq] for h in range(DSA_HEADS)],
                               axis=-1).astype(o_ref.dtype)


def _dsa(qd, qi, wi_t, kd, vd, ki, ck):
    B, _, T, hd = qd.shape
    tq = TQ
    ktop = min(DSA_TOPK, T // 4)
    tri = jnp.asarray(np.tril(np.ones((ck, ck), np.float32), -1), MXU_DTYPE)
    kv_spec = pl.BlockSpec((1, T, hd), lambda b, i: (b, 0, 0))
    v_spec = pl.BlockSpec((1, T, LANES), lambda b, i: (b, 0, 0))
    return pl.pallas_call(
        functools.partial(_dsa_kernel, ck=ck, ktop=ktop),
        out_shape=jax.ShapeDtypeStruct((B, T, DSA_HEADS * hd), MXU_DTYPE),
        grid=(B, T // tq),
        in_specs=[pl.BlockSpec((1, DSA_HEADS, tq, hd), lambda b, i: (b, 0, i, 0)),
                  pl.BlockSpec((1, IDX_HEADS, tq, IDX_DIM), lambda b, i: (b, 0, i, 0)),
                  pl.BlockSpec((1, IDX_HEADS, tq), lambda b, i: (b, 0, i)),
                  kv_spec, v_spec, kv_spec,
                  pl.BlockSpec((ck, ck), lambda b, i: (0, 0))],
        out_specs=pl.BlockSpec((1, tq, DSA_HEADS * hd), lambda b, i: (b, i, 0)),
        scratch_shapes=[pltpu.VMEM((T // ck, ck, tq), I32),
                        pltpu.VMEM((WORD, T // (WORD * SUBLANES), SUBLANES, tq), I32)],
        compiler_params=pltpu.CompilerParams(
            dimension_semantics=("parallel", "parallel"), vmem_limit_bytes=VMEM_LIMIT),
        name="dsa",
    )(qd, qi, wi_t, kd, vd, ki, tri)


def _ffn_kernel(on_ref, od_ref, x_ref, mod_ref, gpm_ref, gpf_ref, gpo_ref,
                woa_ref, wob_ref, wup_ref, wdn_ref, out_ref, *, fc):
    o = _dot(on_ref[0], woa_ref[...]) + _dot(od_ref[0], wob_ref[...])
    x1 = x_ref[0] + mod_ref[0, 2:3, :] * _rms(o, gpm_ref[...])
    h = _rms(x1, gpf_ref[...]) * (1.0 + mod_ref[0, 4:5, :]) + mod_ref[0, 3:4, :]
    hb = h.astype(MXU_DTYPE)
    d_ff = wup_ref.shape[1]
    y = jnp.zeros(x1.shape, F32)
    for c in range(d_ff // fc):
        u = jnp.maximum(_dot(hb, wup_ref[:, c * fc:(c + 1) * fc]), 0.0)
        y = y + _dot((u * u).astype(MXU_DTYPE), wdn_ref[c * fc:(c + 1) * fc, :])
    out_ref[0] = x1 + mod_ref[0, 5:6, :] * _rms(y, gpo_ref[...])


def _ffn(o_nsa, o_dsa, x, mod, g_post_mix, g_pre_ffn, g_post_ffn, w_out, w_up, w_down, tm, fc):
    B, T, D = x.shape
    dn = o_nsa.shape[-1]

    def tok(w):
        return pl.BlockSpec((1, tm, w), lambda b, i: (b, i, 0))

    def whole(a):
        return pl.BlockSpec(a.shape, lambda b, i: (0,) * a.ndim, pipeline_mode=pl.Buffered(1))

    woa, wob = w_out[:dn], w_out[dn:]
    vec = pl.BlockSpec((1, D), lambda b, i: (0, 0))
    return pl.pallas_call(
        functools.partial(_ffn_kernel, fc=fc),
        out_shape=jax.ShapeDtypeStruct((B, T, D), F32),
        grid=(B, T // tm),
        in_specs=[tok(dn), tok(o_dsa.shape[-1]), tok(D),
                  pl.BlockSpec((1, 6, D), lambda b, i: (b, 0, 0)), vec, vec, vec,
                  whole(woa), whole(wob), whole(w_up), whole(w_down)],
        out_specs=tok(D),
        compiler_params=pltpu.CompilerParams(
            dimension_semantics=("parallel", "parallel"), vmem_limit_bytes=VMEM_LIMIT),
        name="ffn",
    )(o_nsa, o_dsa, x, mod, g_post_mix, g_pre_ffn, g_post_ffn, woa, wob, w_up, w_down)


def _permute_w_in(w):
    cols = [w[:, _SEG[n][0]:_SEG[n][1]] for n in _ROPE_ORDER + _PLAIN_ORDER]
    used = sum(c.shape[1] for c in cols)
    cols.append(jnp.zeros((w.shape[0], D_PROJ - used), w.dtype))
    return jnp.concatenate(cols, axis=1).astype(MXU_DTYPE)


def kernel(x, c, positions, w_ada, b_ada, g_pre_mix, g_post_mix, g_pre_ffn, g_post_ffn,
           w_in, cmp_pe_k, cmp_pe_v, cmp_w1_k, cmp_w2_k, cmp_w1_v, cmp_w2_v,
           w_out, w_up, w_down):
    B, T, D = x.shape
    depth = w_ada.shape[0]
    tm = min(512, T)
    tk = min(512, T)
    half = HEAD_DIM // 2
    inv = ROPE_THETA ** (-jnp.arange(half, dtype=F32) / half)
    inv_tile = jnp.tile(inv, LANES // half).reshape(1, LANES)
    pos_f = positions.astype(F32).reshape(B, T, 1)
    ncmp = T // CMP_STRIDE

    for l in range(depth):
        mod = _ada(c, w_ada[l], b_ada[l]).reshape(B, 6, D)
        (qn, kc, vc, ksl, vsl, kw, vw, gates, qd, kd, vd, qi, ki, wi) = _in_proj(
            x, mod, g_pre_mix[l].reshape(1, D), _permute_w_in(w_in[l]), pos_f, inv_tile, tm)
        rows = (B, NSA_KV, ncmp, CMP_STRIDE * HEAD_DIM)
        o_nsa = _nsa(qn, kc.reshape(rows), vc.reshape(rows), ksl, vsl, kw, vw, gates,
                     cmp_pe_k[l].reshape(1, -1), cmp_pe_v[l].reshape(1, -1),
                     cmp_w1_k[l].astype(MXU_DTYPE), cmp_w2_k[l].astype(MXU_DTYPE),
                     cmp_w1_v[l].astype(MXU_DTYPE), cmp_w2_v[l].astype(MXU_DTYPE), tk=tk)
        o_dsa = _dsa(qd, qi, jnp.swapaxes(wi, -1, -2), kd, vd, ki, ck=tk)
        x = _ffn(o_nsa, o_dsa, x, mod, g_post_mix[l].reshape(1, D), g_pre_ffn[l].reshape(1, D),
                 g_post_ffn[l].reshape(1, D), w_out[l].astype(MXU_DTYPE),
                 w_up[l].astype(MXU_DTYPE), w_down[l].astype(MXU_DTYPE), tm=tm, fc=1024)
    return x
```

```python
import functools

import numpy as np
import jax
import jax.numpy as jnp
from jax import lax
from jax.experimental import pallas as pl
from jax.experimental.pallas import tpu as pltpu

HEAD_DIM = 64
NSA_HEADS = 8
NSA_KV = 2
NSA_REP = NSA_HEADS // NSA_KV
DSA_HEADS = 8
IDX_HEADS = 4
IDX_DIM = 64
CMP_LEN = 32
CMP_STRIDE = 16
SLC_LEN = 64
SLC_TOPN = 16
WIN = 512
DSA_TOPK = 256
ROPE_THETA = 10000.0
EPS = 1e-6
NEG = -1e30
MASK_BIAS = -2e30
M_INIT = -1e30

LANES = 128
SUBLANES = 8
MXU_DTYPE = jnp.bfloat16
SOFTMAX_DTYPE = jnp.bfloat16
VMEM_LIMIT = 56 * 1024 * 1024

TQ = 2 * LANES
WORD = 32
F32 = jnp.float32
I32 = jnp.int32
INT_MIN = np.int32(-2 ** 31)

_SEG = dict(q_n=(0, 512), kc=(512, 640), vc=(640, 768), ksl=(768, 896), vsl=(896, 1024),
            kw=(1024, 1152), vw=(1152, 1280), gl=(1280, 1304), q_d=(1304, 1816),
            k_d=(1816, 1880), v_d=(1880, 1944), qi=(1944, 2200), ki=(2200, 2264), wi=(2264, 2268))
_ROPE_ORDER = ("q_n", "kc", "ksl", "kw", "q_d", "qi", "k_d", "ki")
_PLAIN_ORDER = ("vc", "vsl", "vw", "v_d", "gl", "wi")
D_PROJ = 18 * LANES


def _dot(a, b):
    return jnp.dot(a, b, preferred_element_type=F32)


def _dot_nt(a, b):
    return lax.dot_general(a, b, (((1,), (1,)), ((), ())), preferred_element_type=F32)


def _rms(x, g):
    return x * lax.rsqrt(jnp.mean(x * x, axis=-1, keepdims=True) + EPS) * g


def _ada_kernel(c_ref, w_ref, b_ref, o_ref):
    o_ref[...] = _dot(c_ref[...].astype(MXU_DTYPE), w_ref[...].astype(MXU_DTYPE)) + b_ref[...]


def _ada(c, w, b):
    B, D = c.shape
    n = w.shape[1] // D
    return pl.pallas_call(
        _ada_kernel,
        out_shape=jax.ShapeDtypeStruct((B, n * D), F32),
        grid=(n,),
        in_specs=[pl.BlockSpec((B, D), lambda j: (0, 0)),
                  pl.BlockSpec((D, D), lambda j: (0, j)),
                  pl.BlockSpec((1, D), lambda j: (0, j))],
        out_specs=pl.BlockSpec((B, D), lambda j: (0, j)),
        name="ada",
    )(c, w, b.reshape(1, -1))


def _inproj_kernel(x_ref, mod_ref, g_ref, w_ref, pos_ref, inv_ref,
                   qn_ref, kc_ref, vc_ref, ksl_ref, vsl_ref, kw_ref, vw_ref, gate_ref,
                   qd_ref, kd_ref, vd_ref, qi_ref, ki_ref, wi_ref, rows_ref):
    x = x_ref[0]
    tm = x.shape[0]
    h = _rms(x, g_ref[...]) * (1.0 + mod_ref[0, 1:2, :]) + mod_ref[0, 0:1, :]
    proj = _dot(h.astype(MXU_DTYPE), w_ref[...])

    ang = pos_ref[0] * inv_ref[...]
    lane = lax.broadcasted_iota(I32, (tm, LANES), 1)
    first_half = (lane & (HEAD_DIM // 2)) == 0
    cosv = jnp.cos(ang)
    sinv = jnp.sin(ang)
    sin_signed = jnp.where(first_half, -sinv, sinv)

    def rope(j):
        c = proj[:, j * LANES:(j + 1) * LANES]
        swapped = jnp.where(first_half, pltpu.roll(c, LANES - HEAD_DIM // 2, 1),
                            pltpu.roll(c, HEAD_DIM // 2, 1))
        return c * cosv + swapped * sin_signed

    def plain(j):
        return proj[:, j * LANES:(j + 1) * LANES]

    qscale = HEAD_DIM ** -0.5
    dt = qn_ref.dtype
    for j in range(4):
        r = rope(j) * qscale
        qn_ref[0, 2 * j] = r[:, :HEAD_DIM].astype(dt)
        qn_ref[0, 2 * j + 1] = r[:, HEAD_DIM:].astype(dt)
    def put_rows(ref, c):
        rows_ref[...] = c
        for t in range(CMP_STRIDE):
            sub = rows_ref[pl.ds(t, tm // CMP_STRIDE, stride=CMP_STRIDE), :]
            ref[0, 0, :, t * HEAD_DIM:(t + 1) * HEAD_DIM] = sub[:, :HEAD_DIM].astype(dt)
            ref[0, 1, :, t * HEAD_DIM:(t + 1) * HEAD_DIM] = sub[:, HEAD_DIM:].astype(dt)

    put_rows(kc_ref, rope(4))
    for ref, j in ((ksl_ref, 5), (kw_ref, 6)):
        r = rope(j)
        ref[0, 0] = r[:, :HEAD_DIM].astype(dt)
        ref[0, 1] = r[:, HEAD_DIM:].astype(dt)
    for j in range(4):
        r = rope(7 + j) * qscale
        qd_ref[0, 2 * j] = r[:, :HEAD_DIM].astype(dt)
        qd_ref[0, 2 * j + 1] = r[:, HEAD_DIM:].astype(dt)
    for j in range(2):
        r = rope(11 + j)
        qi_ref[0, 2 * j] = r[:, :HEAD_DIM].astype(dt)
        qi_ref[0, 2 * j + 1] = r[:, HEAD_DIM:].astype(dt)
    r = rope(13)
    kd_ref[0] = r[:, :HEAD_DIM].astype(dt)
    ki_ref[0] = r[:, HEAD_DIM:].astype(dt)
    put_rows(vc_ref, plain(14))
    def with_ones(c):
        return jnp.where(lane < HEAD_DIM, c, 1.0).astype(dt)

    for ref, j in ((vsl_ref, 15), (vw_ref, 16)):
        v = plain(j)
        ref[0, 0] = with_ones(v)
        ref[0, 1] = with_ones(pltpu.roll(v, HEAD_DIM, 1))
    last = plain(17)
    vd_ref[0] = with_ones(last)
    ng = NSA_REP * 3
    gate_ref[0] = jax.nn.sigmoid(last)
    idx_scale = (IDX_HEADS ** -0.5) * (IDX_DIM ** -0.5)
    wi0 = HEAD_DIM + 2 * ng
    wi_ref[0] = last.T[wi0:wi0 + IDX_HEADS, :] * idx_scale


def _in_proj(x, mod, g_pre, w_perm, pos_f, inv_tile, tm):
    B, T, D = x.shape
    dt = MXU_DTYPE
    hd = HEAD_DIM

    def heads(n, w=hd):
        return (jax.ShapeDtypeStruct((B, n, T, w), dt),
                pl.BlockSpec((1, n, tm, w), lambda b, i: (b, 0, i, 0)))

    def flat(w, d=dt):
        return (jax.ShapeDtypeStruct((B, T, w), d),
                pl.BlockSpec((1, tm, w), lambda b, i: (b, i, 0)))

    row_w = CMP_STRIDE * hd
    row_view = (jax.ShapeDtypeStruct((B, NSA_KV, T // CMP_STRIDE, row_w), dt),
                pl.BlockSpec((1, NSA_KV, tm // CMP_STRIDE, row_w), lambda b, i: (b, 0, i, 0)))
    outs = [heads(NSA_HEADS),
            row_view, row_view,
            heads(NSA_KV), heads(NSA_KV, LANES),
            heads(NSA_KV), heads(NSA_KV, LANES),
            flat(LANES, F32),
            heads(DSA_HEADS),
            flat(hd), flat(LANES),
            heads(IDX_HEADS),
            flat(hd),
            (jax.ShapeDtypeStruct((B, IDX_HEADS, T), F32),
             pl.BlockSpec((1, IDX_HEADS, tm), lambda b, i: (b, 0, i)))]
    return pl.pallas_call(
        _inproj_kernel,
        out_shape=[o[0] for o in outs],
        grid=(B, T // tm),
        in_specs=[pl.BlockSpec((1, tm, D), lambda b, i: (b, i, 0)),
                  pl.BlockSpec((1, 6, D), lambda b, i: (b, 0, 0)),
                  pl.BlockSpec((1, D), lambda b, i: (0, 0)),
                  pl.BlockSpec((D, D_PROJ), lambda b, i: (0, 0)),
                  pl.BlockSpec((1, tm, 1), lambda b, i: (b, i, 0)),
                  pl.BlockSpec((1, LANES), lambda b, i: (0, 0))],
        out_specs=[o[1] for o in outs],
        scratch_shapes=[pltpu.VMEM((tm, LANES), F32)],
        compiler_params=pltpu.CompilerParams(
            dimension_semantics=("parallel", "parallel"), vmem_limit_bytes=VMEM_LIMIT),
        name="in_proj",
    )(x, mod, g_pre, w_perm, pos_f, inv_tile)


def _softmax_probs(s, bias, m_prev):
    tq, n = bias.shape
    sb = (s.astype(SOFTMAX_DTYPE).reshape(-1, tq, n) + bias.astype(SOFTMAX_DTYPE)[None]).reshape(s.shape)
    m_new = jnp.maximum(m_prev, jnp.max(sb, axis=-1, keepdims=True).astype(F32))
    return m_new, jnp.exp(sb - m_new.astype(SOFTMAX_DTYPE))


def _softmax_step(s, bias, m_prev, acc_prev, v1):
    m_new, p = _softmax_probs(s, bias, m_prev)
    acc_new = jnp.exp(m_prev - m_new) * acc_prev + _dot(p.astype(v1.dtype), v1)
    return m_new, acc_new


def _normalize(acc):
    return acc[:, :HEAD_DIM] * (1.0 / acc[:, HEAD_DIM:HEAD_DIM + 1])


def _nsa_kernel(q_ref, kc_ref, vc_ref, ksl_ref, vsl_ref, kw_ref, vw_ref, gate_ref,
                pek_ref, pev_ref, w1k_ref, w2k_ref, w1v_ref, w2v_ref, e_ref, ovlt_ref,
                o_ref, kcmp_ref, vcmp_ref, *, tk, seq):
    tq = TQ
    qi = pl.program_id(1)
    s0 = qi * tq
    ncmp = seq // CMP_STRIDE
    nslc = seq // SLC_LEN
    nsel = min(SLC_TOPN, nslc)
    rows = NSA_REP * tq
    hd = HEAD_DIM
    groups = range(NSA_KV)

    @pl.when(qi == 0)
    def _():
        half = CMP_STRIDE * hd

        def compress(r, pe_ref, w1_ref, w2_ref):
            a = _dot(r, w1_ref[:half, :])
            b = _dot(r, w1_ref[half:, :])
            pe = jnp.broadcast_to(pe_ref[...], (SUBLANES, CMP_LEN * hd)).astype(MXU_DTYPE)
            bias = _dot(pe, w1_ref[...])[0:1, :]
            pre = a + pltpu.roll(b, ncmp - 1, 0) + bias
            return _dot(jax.nn.gelu(pre).astype(MXU_DTYPE), w2_ref[...])

        for g in groups:
            kcmp_ref[g] = compress(kc_ref[0, g], pek_ref, w1k_ref, w2k_ref).astype(kcmp_ref.dtype)
            vcmp_ref[g] = compress(vc_ref[0, g], pev_ref, w1v_ref, w2v_ref).astype(vcmp_ref.dtype)

    q = [q_ref[0, g * NSA_REP:(g + 1) * NSA_REP].reshape(rows, hd) for g in groups]

    span = WIN + tq
    w0 = pl.multiple_of(jnp.maximum(s0 - WIN, 0), tq)
    s_c = [_dot_nt(q[g], kcmp_ref[g]).reshape(NSA_REP, tq, ncmp) for g in groups]
    s_w = [_dot_nt(q[g], kw_ref[0, g, pl.ds(w0, span), :]) for g in groups]

    t_c = s0 + lax.broadcasted_iota(I32, (1, tq, ncmp), 1)
    n_c = lax.broadcasted_iota(I32, (1, tq, ncmp), 2)
    mask_c = (n_c * CMP_STRIDE + (CMP_LEN - 1)) <= t_c
    maskf_c = jnp.where(mask_c, 1.0, 0.0)
    o_cmp, imp = [], []
    for g in groups:
        s_m = jnp.where(mask_c, s_c[g], NEG)
        p_c = jnp.exp(s_m - jnp.max(s_m, axis=-1, keepdims=True)) * maskf_c
        l_c = jnp.sum(p_c, axis=-1, keepdims=True)
        p_c = p_c * (1.0 / jnp.maximum(l_c, 1e-30))
        o_cmp.append(_dot(p_c.reshape(rows, ncmp).astype(MXU_DTYPE), vcmp_ref[g]))
        p_sum = jnp.sum(p_c, axis=0)
        p_hi = p_sum.astype(MXU_DTYPE)
        p_lo = (p_sum - p_hi.astype(F32)).astype(MXU_DTYPE)
        imp.append(_dot_nt(ovlt_ref[...], p_hi) + _dot_nt(ovlt_ref[...], p_lo))

    kpos = w0 + lax.broadcasted_iota(I32, (tq, span), 1)
    tt = s0 + lax.broadcasted_iota(I32, (tq, span), 0)
    bias_w = jnp.where(kpos <= tt, jnp.where(kpos > tt - WIN, 0.0, MASK_BIAS), MASK_BIAS)
    o_win = []
    for g in groups:
        _, p_w = _softmax_probs(s_w[g], bias_w, jnp.full((rows, 1), M_INIT, F32))
        o_win.append(_dot(p_w.astype(MXU_DTYPE), vw_ref[0, g, pl.ds(w0, span), :]))

    j_idx = lax.broadcasted_iota(I32, (nslc, tq), 0)
    t_s = s0 + lax.broadcasted_iota(I32, (nslc, tq), 1)
    cur = lax.shift_right_logical(t_s, int(np.log2(SLC_LEN)))
    blk_bias = []
    for g in groups:
        sc = jnp.where(j_idx == 0, jnp.inf,
                       jnp.where(j_idx == cur, jnp.inf, jnp.where(j_idx == cur - 1, jnp.inf, imp[g])))
        sc = jnp.where(j_idx <= cur, sc, -jnp.inf)
        rank = jnp.zeros((nslc, tq), I32)
        for i in range(nslc):
            ci = sc[i:i + 1, :]
            rank = rank + jnp.where(ci > sc, 1, jnp.where(ci == sc, jnp.where(j_idx > i, 1, 0), 0))
        bb = jnp.where(rank < nsel, jnp.where(j_idx <= cur, 0.0, MASK_BIAS), MASK_BIAS)
        blk_bias.append(bb.T.astype(MXU_DTYPE))

    def sel_body(kt, carry):
        off = pl.multiple_of(kt * tk, tk)
        kpos = off + lax.broadcasted_iota(I32, (tq, tk), 1)
        tt = s0 + lax.broadcasted_iota(I32, (tq, tk), 0)
        out = []
        for g in groups:
            m, acc = carry[g]
            k = ksl_ref[0, g, pl.ds(off, tk), :]
            v1 = vsl_ref[0, g, pl.ds(off, tk), :]
            bias = jnp.where(kpos <= tt, _dot(blk_bias[g], e_ref[kt]), MASK_BIAS)
            out.append(_softmax_step(_dot_nt(q[g], k), bias, m, acc, v1))
        return tuple(out)

    init = tuple((jnp.full((rows, 1), M_INIT, F32), jnp.zeros((rows, LANES), F32)) for _ in groups)
    sel = lax.fori_loop(0, s0 // tk + 1, sel_body, init)

    gt = gate_ref[0]
    outs = []
    for g in groups:
        for r in range(NSA_REP):
            sl = slice(r * tq, (r + 1) * tq)
            lane0 = HEAD_DIM + NSA_REP * 3 * g + 3 * r
            acc_s, acc_w = sel[g][1][sl], o_win[g][sl]
            w_s = gt * (1.0 / acc_s)
            w_w = gt * (1.0 / acc_w)
            outs.append(gt[:, lane0:lane0 + 1] * o_cmp[g][sl]
                        + w_s[:, lane0 + 1:lane0 + 2] * acc_s[:, :HEAD_DIM]
                        + w_w[:, lane0 + 2:lane0 + 3] * acc_w[:, :HEAD_DIM])
    o_ref[0] = jnp.concatenate(outs, axis=-1).astype(o_ref.dtype)


def _nsa(qn, kc_rows, vc_rows, ksl, vsl, kw, vw, gates, pe_k, pe_v, w1k, w2k, w1v, w2v, tk):
    B, _, T, hd = qn.shape
    tq = TQ
    ncmp = T // CMP_STRIDE
    nslc = T // SLC_LEN
    key = np.arange(T)
    e = (key[None, :] // SLC_LEN == np.arange(nslc)[:, None]).astype(np.float32)
    e = jnp.asarray(e.reshape(nslc, T // tk, tk).transpose(1, 0, 2), MXU_DTYPE)
    cmp_start = np.arange(ncmp) * CMP_STRIDE
    cmp_end = cmp_start + CMP_LEN - 1
    slc_start = np.arange(nslc) * SLC_LEN
    slc_end = slc_start + SLC_LEN - 1
    ovl = ((cmp_start[:, None] <= slc_end[None, :]) & (cmp_end[:, None] >= slc_start[None, :]))
    ovl[ncmp - 1, :] = False
    ovl_t = jnp.asarray(ovl.T.astype(np.float32), MXU_DTYPE)

    def both(shape):
        return pl.BlockSpec((1, NSA_KV) + shape, lambda b, i: (b, 0) + (0,) * len(shape))

    def whole(a):
        return pl.BlockSpec(a.shape, lambda b, i: (0,) * a.ndim)

    consts = (pe_k, pe_v, w1k, w2k, w1v, w2v, e, ovl_t)
    return pl.pallas_call(
        functools.partial(_nsa_kernel, tk=tk, seq=T),
        out_shape=jax.ShapeDtypeStruct((B, T, NSA_HEADS * hd), MXU_DTYPE),
        grid=(B, T // tq),
        in_specs=[pl.BlockSpec((1, NSA_HEADS, tq, hd), lambda b, i: (b, 0, i, 0)),
                  both((ncmp, CMP_STRIDE * hd)), both((ncmp, CMP_STRIDE * hd)),
                  both((T, hd)), both((T, LANES)), both((T, hd)), both((T, LANES)),
                  pl.BlockSpec((1, tq, LANES), lambda b, i: (b, i, 0))]
                 + [whole(a) for a in consts],
        out_specs=pl.BlockSpec((1, tq, NSA_HEADS * hd), lambda b, i: (b, i, 0)),
        scratch_shapes=[pltpu.VMEM((NSA_KV, ncmp, hd), MXU_DTYPE),
                        pltpu.VMEM((NSA_KV, ncmp, hd), MXU_DTYPE)],
        compiler_params=pltpu.CompilerParams(
            dimension_semantics=("parallel", "arbitrary"), vmem_limit_bytes=VMEM_LIMIT),
        name="nsa",
    )(qn, kc_rows, vc_rows, ksl, vsl, kw, vw, gates, *consts)


def _bit_transpose32(rows):
    a = list(rows)
    j, m = 16, 0x0000FFFF
    while j:
        for k in range(WORD):
            if k & j == 0:
                t = (a[k] ^ lax.shift_right_logical(a[k + j], np.int32(j))) & np.int32(m)
                a[k] = a[k] ^ t
                a[k + j] = a[k + j] ^ jnp.left_shift(t, np.int32(j))
        j >>= 1
        m ^= (m << j) & 0xFFFFFFFF
    return a


def _dsa_kernel(qd_ref, qi_ref, wi_ref, kd_ref, vd_ref, ki_ref, tri_ref, o_ref, key_ref, plane_ref,
                *, ck, ktop):
    tq = TQ
    s0 = pl.program_id(1) * tq
    n_kc = s0 // ck + 1
    hd = HEAD_DIM
    rows = DSA_HEADS * tq
    gpc = ck // (WORD * SUBLANES)
    n_groups = plane_ref.shape[1]

    w = wi_ref[0]
    q_idx = qi_ref[0].reshape(IDX_HEADS * tq, IDX_DIM)
    krow = lax.broadcasted_iota(I32, (ck, tq), 0)
    t_q = s0 + lax.broadcasted_iota(I32, (ck, tq), 1)

    def score_body(c, carry):
        off = pl.multiple_of(c * ck, ck)
        lg = _dot_nt(ki_ref[0, pl.ds(off, ck), :], q_idx)
        sc = w[0:1, :] * jnp.maximum(lg[:, :tq], 0.0)
        for h in range(1, IDX_HEADS):
            sc = sc + w[h:h + 1, :] * jnp.maximum(lg[:, h * tq:(h + 1) * tq], 0.0)
        sc = jnp.where(sc == 0.0, 0.0, sc)
        bits = pltpu.bitcast(sc, I32)
        key = jnp.where(bits < 0, bits ^ np.int32(0x7FFFFFFF), bits)
        key = jnp.where(off + krow <= t_q, key, INT_MIN)
        key_ref[c] = key
        for g in range(gpc):
            slabs = [key[(g * WORD + j) * SUBLANES:(g * WORD + j + 1) * SUBLANES, :] ^ INT_MIN
                     for j in range(WORD)]
            for i, plane in enumerate(_bit_transpose32(slabs)):
                plane_ref[i, c * gpc + g] = plane
        return carry

    lax.fori_loop(0, n_kc, score_body, 0)

    def clear_body(c, carry):
        plane_ref[:, pl.ds(c * gpc, gpc)] = jnp.zeros((WORD, gpc, SUBLANES, tq), I32)
        return carry

    lax.fori_loop(n_kc, n_groups // gpc, clear_body, 0)

    def bit_body(i, carry):
        cand, need, thr_u = carry
        ones = cand & plane_ref[i]
        cnt = jnp.sum(jnp.sum(lax.population_count(ones), axis=0), axis=0, keepdims=True)
        take = cnt >= need
        cand = jnp.where(take, ones, cand ^ ones)
        need = jnp.where(take, need, need - cnt)
        thr_u = thr_u | jnp.where(take, jnp.left_shift(np.int32(1), np.int32(WORD - 1) - i), 0)
        return cand, need, thr_u

    cand, need, thr_u = lax.fori_loop(
        0, WORD, bit_body,
        (jnp.full((n_groups, SUBLANES, tq), -1, I32), jnp.full((1, tq), ktop, I32),
         jnp.zeros((1, tq), I32)))
    thr = thr_u ^ INT_MIN
    n_eq = jnp.sum(jnp.sum(lax.population_count(cand), axis=0), axis=0, keepdims=True)
    need_eq = need.astype(F32)
    exact_cut = jnp.max(jnp.where(thr_u == 0, 1, jnp.abs(n_eq - need))) == 0

    q = qd_ref[0].reshape(rows, hd)

    def att_body(c, carry):
        m, acc, eq_seen = carry
        off = pl.multiple_of(c * ck, ck)
        key = key_ref[c]

        def cut_bias():
            return jnp.where(key >= thr, 0.0, MASK_BIAS), eq_seen

        def tie_bias():
            eq_f = jnp.where(key == thr, 1.0, 0.0)
            before = _dot(tri_ref[...], eq_f.astype(MXU_DTYPE)) + eq_seen
            take = jnp.where(key > thr, 0.0,
                             jnp.where(key == thr, jnp.where(before < need_eq, 0.0, MASK_BIAS), MASK_BIAS))
            return (jnp.where(off + krow <= t_q, take, MASK_BIAS),
                    eq_seen + jnp.sum(eq_f, axis=0, keepdims=True))

        bias_t, eq_seen = lax.cond(exact_cut, cut_bias, tie_bias)
        bias = bias_t.T
        k = kd_ref[0, pl.ds(off, ck), :]
        v1 = vd_ref[0, pl.ds(off, ck), :]
        m, acc = _softmax_step(_dot_nt(q, k), bias, m, acc, v1)
        return m, acc, eq_seen

    init = (jnp.full((rows, 1), M_INIT, F32), jnp.zeros((rows, LANES), F32),
            jnp.zeros((1, tq), F32))
    _, acc, _ = lax.fori_loop(0, n_kc, att_body, init)
    o = _normalize(acc)
    o_ref[0] = jnp.concatenate([o[h * tq:(h + 1) * tq] for h in range(DSA_HEADS)],
                               axis=-1).astype(o_ref.dtype)


def _dsa(qd, qi, wi_t, kd, vd, ki, ck):
    B, _, T, hd = qd.shape
    tq = TQ
    ktop = min(DSA_TOPK, T // 4)
    tri = jnp.asarray(np.tril(np.ones((ck, ck), np.float32), -1), MXU_DTYPE)
    kv_spec = pl.BlockSpec((1, T, hd), lambda b, i: (b, 0, 0))
    v_spec = pl.BlockSpec((1, T, LANES), lambda b, i: (b, 0, 0))
    return pl.pallas_call(
        functools.partial(_dsa_kernel, ck=ck, ktop=ktop),
        out_shape=jax.ShapeDtypeStruct((B, T, DSA_HEADS * hd), MXU_DTYPE),
        grid=(B, T // tq),
        in_specs=[pl.BlockSpec((1, DSA_HEADS, tq, hd), lambda b, i: (b, 0, i, 0)),
                  pl.BlockSpec((1, IDX_HEADS, tq, IDX_DIM), lambda b, i: (b, 0, i, 0)),
                  pl.BlockSpec((1, IDX_HEADS, tq), lambda b, i: (b, 0, i)),
                  kv_spec, v_spec, kv_spec,
                  pl.BlockSpec((ck, ck), lambda b, i: (0, 0))],
        out_specs=pl.BlockSpec((1, tq, DSA_HEADS * hd), lambda b, i: (b, i, 0)),
        scratch_shapes=[pltpu.VMEM((T // ck, ck, tq), I32),
                        pltpu.VMEM((WORD, T // (WORD * SUBLANES), SUBLANES, tq), I32)],
        compiler_params=pltpu.CompilerParams(
            dimension_semantics=("parallel", "parallel"), vmem_limit_bytes=VMEM_LIMIT),
        name="dsa",
    )(qd, qi, wi_t, kd, vd, ki, tri)


def _ffn_kernel(on_ref, od_ref, x_ref, mod_ref, gpm_ref, gpf_ref, gpo_ref,
                woa_ref, wob_ref, wup_ref, wdn_ref, out_ref, *, fc):
    o = _dot(on_ref[0], woa_ref[...]) + _dot(od_ref[0], wob_ref[...])
    x1 = x_ref[0] + mod_ref[0, 2:3, :] * _rms(o, gpm_ref[...])
    h = _rms(x1, gpf_ref[...]) * (1.0 + mod_ref[0, 4:5, :]) + mod_ref[0, 3:4, :]
    hb = h.astype(MXU_DTYPE)
    d_ff = wup_ref.shape[1]
    y = jnp.zeros(x1.shape, F32)
    for c in range(d_ff // fc):
        u = jnp.maximum(_dot(hb, wup_ref[:, c * fc:(c + 1) * fc]), 0.0)
        y = y + _dot((u * u).astype(MXU_DTYPE), wdn_ref[c * fc:(c + 1) * fc, :])
    out_ref[0] = x1 + mod_ref[0, 5:6, :] * _rms(y, gpo_ref[...])


def _ffn(o_nsa, o_dsa, x, mod, g_post_mix, g_pre_ffn, g_post_ffn, w_out, w_up, w_down, tm, fc):
    B, T, D = x.shape
    dn = o_nsa.shape[-1]

    def tok(w):
        return pl.BlockSpec((1, tm, w), lambda b, i: (b, i, 0))

    def whole(a):
        return pl.BlockSpec(a.shape, lambda b, i: (0,) * a.ndim, pipeline_mode=pl.Buffered(1))

    woa, wob = w_out[:dn], w_out[dn:]
    vec = pl.BlockSpec((1, D), lambda b, i: (0, 0))
    return pl.pallas_call(
        functools.partial(_ffn_kernel, fc=fc),
        out_shape=jax.ShapeDtypeStruct((B, T, D), F32),
        grid=(B, T // tm),
        in_specs=[tok(dn), tok(o_dsa.shape[-1]), tok(D),
                  pl.BlockSpec((1, 6, D), lambda b, i: (b, 0, 0)), vec, vec, vec,
                  whole(woa), whole(wob), whole(w_up), whole(w_down)],
        out_specs=tok(D),
        compiler_params=pltpu.CompilerParams(
            dimension_semantics=("parallel", "parallel"), vmem_limit_bytes=VMEM_LIMIT),
        name="ffn",
    )(o_nsa, o_dsa, x, mod, g_post_mix, g_pre_ffn, g_post_ffn, woa, wob, w_up, w_down)


def _permute_w_in(w):
    cols = [w[:, _SEG[n][0]:_SEG[n][1]] for n in _ROPE_ORDER + _PLAIN_ORDER]
    used = sum(c.shape[1] for c in cols)
    cols.append(jnp.zeros((w.shape[0], D_PROJ - used), w.dtype))
    return jnp.concatenate(cols, axis=1).astype(MXU_DTYPE)


def kernel(x, c, positions, w_ada, b_ada, g_pre_mix, g_post_mix, g_pre_ffn, g_post_ffn,
           w_in, cmp_pe_k, cmp_pe_v, cmp_w1_k, cmp_w2_k, cmp_w1_v, cmp_w2_v,
           w_out, w_up, w_down):
    B, T, D = x.shape
    depth = w_ada.shape[0]
    tm = min(512, T)
    tk = min(512, T)
    half = HEAD_DIM // 2
    inv = ROPE_THETA ** (-jnp.arange(half, dtype=F32) / half)
    inv_tile = jnp.tile(inv, LANES // half).reshape(1, LANES)
    pos_f = positions.astype(F32).reshape(B, T, 1)
    ncmp = T // CMP_STRIDE

    for l in range(depth):
        mod = _ada(c, w_ada[l], b_ada[l]).reshape(B, 6, D)
        (qn, kc, vc, ksl, vsl, kw, vw, gates, qd, kd, vd, qi, ki, wi) = _in_proj(
            x, mod, g_pre_mix[l].reshape(1, D), _permute_w_in(w_in[l]), pos_f, inv_tile, tm)
        o_nsa = _nsa(qn, kc, vc, ksl, vsl, kw, vw, gates,
                     cmp_pe_k[l].reshape(1, -1), cmp_pe_v[l].reshape(1, -1),
                     cmp_w1_k[l].astype(MXU_DTYPE), cmp_w2_k[l].astype(MXU_DTYPE),
                     cmp_w1_v[l].astype(MXU_DTYPE), cmp_w2_v[l].astype(MXU_DTYPE), tk=tk)
        o_dsa = _dsa(qd, qi, wi, kd, vd, ki, ck=tk)
        x = _ffn(o_nsa, o_dsa, x, mod, g_post_mix[l].reshape(1, D), g_pre_ffn[l].reshape(1, D),
                 g_post_ffn[l].reshape(1, D), w_out[l].astype(MXU_DTYPE),
                 w_up[l].astype(MXU_DTYPE), w_down[l].astype(MXU_DTYPE), tm=tm, fc=1024)
    return x
```

```python
import functools

import numpy as np
import jax
import jax.numpy as jnp
from jax import lax
from jax.experimental import pallas as pl
from jax.experimental.pallas import tpu as pltpu

HEAD_DIM = 64
NSA_HEADS = 8
NSA_KV = 2
NSA_REP = NSA_HEADS // NSA_KV
DSA_HEADS = 8
IDX_HEADS = 4
IDX_DIM = 64
CMP_LEN = 32
CMP_STRIDE = 16
SLC_LEN = 64
SLC_TOPN = 16
WIN = 512
DSA_TOPK = 256
ROPE_THETA = 10000.0
EPS = 1e-6
NEG = -1e30
MASK_BIAS = -2e30
M_INIT = -1e30

LANES = 128
SUBLANES = 8
MXU_DTYPE = jnp.bfloat16
SOFTMAX_DTYPE = jnp.bfloat16
VMEM_LIMIT = 56 * 1024 * 1024

TQ = 2 * LANES
TM = 512
TK = 512
FFN_CHUNK = 1024
WORD = 32
F32 = jnp.float32
I32 = jnp.int32
INT_MIN = np.int32(-2 ** 31)

_SEG = dict(q_n=(0, 512), kc=(512, 640), vc=(640, 768), ksl=(768, 896), vsl=(896, 1024),
            kw=(1024, 1152), vw=(1152, 1280), gl=(1280, 1304), q_d=(1304, 1816),
            k_d=(1816, 1880), v_d=(1880, 1944), qi=(1944, 2200), ki=(2200, 2264), wi=(2264, 2268))
_ROPE_ORDER = ("q_n", "kc", "ksl", "kw", "q_d", "qi", "k_d", "ki")
_PLAIN_ORDER = ("vc", "vsl", "vw", "v_d", "gl", "wi")
D_PROJ = 18 * LANES


def _dot(a, b):
    return jnp.dot(a, b, preferred_element_type=F32)


def _dot_nt(a, b):
    return lax.dot_general(a, b, (((1,), (1,)), ((), ())), preferred_element_type=F32)


def _rms(x, g):
    return x * lax.rsqrt(jnp.mean(x * x, axis=-1, keepdims=True) + EPS) * g


def _ada_kernel(c_ref, w_ref, b_ref, o_ref):
    o_ref[...] = _dot(c_ref[...].astype(MXU_DTYPE), w_ref[...].astype(MXU_DTYPE)) + b_ref[...]


def _ada(c, w, b):
    B, D = c.shape
    n = w.shape[1] // D
    return pl.pallas_call(
        _ada_kernel,
        out_shape=jax.ShapeDtypeStruct((B, n * D), F32),
        grid=(n,),
        in_specs=[pl.BlockSpec((B, D), lambda j: (0, 0)),
                  pl.BlockSpec((D, D), lambda j: (0, j)),
                  pl.BlockSpec((1, D), lambda j: (0, j))],
        out_specs=pl.BlockSpec((B, D), lambda j: (0, j)),
        name="ada",
    )(c, w, b.reshape(1, -1))


def _inproj_kernel(x_ref, mod_ref, g_ref, w_ref, pos_ref, inv_ref,
                   qn_ref, kc_ref, vc_ref, ksl_ref, vsl_ref, kw_ref, vw_ref, gate_ref,
                   qd_ref, kd_ref, vd_ref, qi_ref, ki_ref, wi_ref, rows_ref):
    x = x_ref[0]
    tm = x.shape[0]
    h = _rms(x, g_ref[...] * (1.0 + mod_ref[0, 1:2, :])) + mod_ref[0, 0:1, :]
    proj = _dot(h.astype(MXU_DTYPE), w_ref[...])

    ang = pos_ref[0] * inv_ref[...]
    lane = lax.broadcasted_iota(I32, (tm, LANES), 1)
    first_half = (lane & (HEAD_DIM // 2)) == 0
    cosv = jnp.cos(ang)
    sinv = jnp.sin(ang)
    sin_signed = jnp.where(first_half, -sinv, sinv)

    def rope(j):
        c = proj[:, j * LANES:(j + 1) * LANES]
        swapped = jnp.where(first_half, pltpu.roll(c, LANES - HEAD_DIM // 2, 1),
                            pltpu.roll(c, HEAD_DIM // 2, 1))
        return c * cosv + swapped * sin_signed

    def plain(j):
        return proj[:, j * LANES:(j + 1) * LANES]

    dt = qn_ref.dtype
    for j in range(4):
        r = rope(j)
        qn_ref[0, 2 * j] = r[:, :HEAD_DIM].astype(dt)
        qn_ref[0, 2 * j + 1] = r[:, HEAD_DIM:].astype(dt)
    def put_rows(ref, c):
        rows_ref[...] = c
        for t in range(CMP_STRIDE):
            sub = rows_ref[pl.ds(t, tm // CMP_STRIDE, stride=CMP_STRIDE), :]
            ref[0, 0, :, t * HEAD_DIM:(t + 1) * HEAD_DIM] = sub[:, :HEAD_DIM].astype(dt)
            ref[0, 1, :, t * HEAD_DIM:(t + 1) * HEAD_DIM] = sub[:, HEAD_DIM:].astype(dt)

    put_rows(kc_ref, rope(4))
    for ref, j in ((ksl_ref, 5), (kw_ref, 6)):
        r = rope(j)
        ref[0, 0] = r[:, :HEAD_DIM].astype(dt)
        ref[0, 1] = r[:, HEAD_DIM:].astype(dt)
    for j in range(4):
        r = rope(7 + j)
        qd_ref[0, 2 * j] = r[:, :HEAD_DIM].astype(dt)
        qd_ref[0, 2 * j + 1] = r[:, HEAD_DIM:].astype(dt)
    for j in range(2):
        r = rope(11 + j)
        qi_ref[0, 2 * j] = r[:, :HEAD_DIM].astype(dt)
        qi_ref[0, 2 * j + 1] = r[:, HEAD_DIM:].astype(dt)
    r = rope(13)
    kd_ref[0] = r[:, :HEAD_DIM].astype(dt)
    ki_ref[0] = r[:, HEAD_DIM:].astype(dt)
    put_rows(vc_ref, plain(14))
    def with_ones(c):
        return jnp.where(lane < HEAD_DIM, c, 1.0).astype(dt)

    for ref, j in ((vsl_ref, 15), (vw_ref, 16)):
        v = plain(j)
        ref[0, 0] = with_ones(v)
        ref[0, 1] = with_ones(pltpu.roll(v, HEAD_DIM, 1))
    last = plain(17)
    vd_ref[0] = with_ones(last)
    ng = NSA_REP * 3
    gate_ref[0] = jax.nn.sigmoid(last)
    idx_scale = (IDX_HEADS ** -0.5) * (IDX_DIM ** -0.5)
    wi0 = HEAD_DIM + 2 * ng
    wi_ref[0] = last.T[wi0:wi0 + IDX_HEADS, :] * idx_scale


def _in_proj(x, mod, g_pre, w_perm, pos_f, inv_tile, tm):
    B, T, D = x.shape
    dt = MXU_DTYPE
    hd = HEAD_DIM

    def heads(n, w=hd):
        return (jax.ShapeDtypeStruct((B, n, T, w), dt),
                pl.BlockSpec((1, n, tm, w), lambda b, i: (b, 0, i, 0)))

    def flat(w, d=dt):
        return (jax.ShapeDtypeStruct((B, T, w), d),
                pl.BlockSpec((1, tm, w), lambda b, i: (b, i, 0)))

    row_w = CMP_STRIDE * hd
    row_view = (jax.ShapeDtypeStruct((B, NSA_KV, T // CMP_STRIDE, row_w), dt),
                pl.BlockSpec((1, NSA_KV, tm // CMP_STRIDE, row_w), lambda b, i: (b, 0, i, 0)))
    outs = [heads(NSA_HEADS),
            row_view, row_view,
            heads(NSA_KV), heads(NSA_KV, LANES),
            heads(NSA_KV), heads(NSA_KV, LANES),
            flat(LANES, F32),
            heads(DSA_HEADS),
            flat(hd), flat(LANES),
            heads(IDX_HEADS),
            flat(hd),
            (jax.ShapeDtypeStruct((B, IDX_HEADS, T), F32),
             pl.BlockSpec((1, IDX_HEADS, tm), lambda b, i: (b, 0, i)))]
    return pl.pallas_call(
        _inproj_kernel,
        out_shape=[o[0] for o in outs],
        grid=(B, T // tm),
        in_specs=[pl.BlockSpec((1, tm, D), lambda b, i: (b, i, 0)),
                  pl.BlockSpec((1, 6, D), lambda b, i: (b, 0, 0)),
                  pl.BlockSpec((1, D), lambda b, i: (0, 0)),
                  pl.BlockSpec((D, D_PROJ), lambda b, i: (0, 0)),
                  pl.BlockSpec((1, tm, 1), lambda b, i: (b, i, 0)),
                  pl.BlockSpec((1, LANES), lambda b, i: (0, 0))],
        out_specs=[o[1] for o in outs],
        scratch_shapes=[pltpu.VMEM((tm, LANES), F32)],
        compiler_params=pltpu.CompilerParams(
            dimension_semantics=("parallel", "parallel"), vmem_limit_bytes=VMEM_LIMIT),
        name="in_proj",
    )(x, mod, g_pre, w_perm, pos_f, inv_tile)


def _softmax_probs(s, bias, m_prev):
    tq, n = bias.shape
    sb = (s.astype(SOFTMAX_DTYPE).reshape(-1, tq, n) + bias.astype(SOFTMAX_DTYPE)[None]).reshape(s.shape)
    m_new = jnp.maximum(m_prev, jnp.max(sb, axis=-1, keepdims=True).astype(F32))
    return m_new, jnp.exp(sb - m_new.astype(SOFTMAX_DTYPE))


def _softmax_step(s, bias, m_prev, acc_prev, v1):
    m_new, p = _softmax_probs(s, bias, m_prev)
    acc_new = jnp.exp(m_prev - m_new) * acc_prev + _dot(p.astype(v1.dtype), v1)
    return m_new, acc_new


def _normalize(acc):
    return acc[:, :HEAD_DIM] * (1.0 / acc[:, HEAD_DIM:HEAD_DIM + 1])


def _nsa_kernel(q_ref, kc_ref, vc_ref, ksl_ref, vsl_ref, kw_ref, vw_ref, gate_ref,
                pek_ref, pev_ref, w1k_ref, w2k_ref, w1v_ref, w2v_ref, e_ref, ovlt_ref,
                o_ref, kcmp_ref, vcmp_ref, *, tk, seq):
    tq = TQ
    qi = pl.program_id(1)
    s0 = qi * tq
    ncmp = seq // CMP_STRIDE
    nslc = seq // SLC_LEN
    nsel = min(SLC_TOPN, nslc)
    rows = NSA_REP * tq
    hd = HEAD_DIM
    groups = range(NSA_KV)

    @pl.when(qi == 0)
    def _():
        half = CMP_STRIDE * hd

        def compress(r, pe_ref, w1_ref, w2_ref):
            a = _dot(r, w1_ref[:half, :])
            b = _dot(r, w1_ref[half:, :])
            pe = jnp.broadcast_to(pe_ref[...], (SUBLANES, CMP_LEN * hd)).astype(MXU_DTYPE)
            bias = _dot(pe, w1_ref[...])[0:1, :]
            pre = a + pltpu.roll(b, ncmp - 1, 0) + bias
            return _dot(jax.nn.gelu(pre).astype(MXU_DTYPE), w2_ref[...])

        for g in groups:
            kcmp_ref[g] = compress(kc_ref[0, g], pek_ref, w1k_ref, w2k_ref).astype(kcmp_ref.dtype)
            vcmp_ref[g] = compress(vc_ref[0, g], pev_ref, w1v_ref, w2v_ref).astype(vcmp_ref.dtype)

    q = [q_ref[0, g * NSA_REP:(g + 1) * NSA_REP].reshape(rows, hd) for g in groups]

    span = WIN + tq
    w0 = pl.multiple_of(jnp.maximum(s0 - WIN, 0), tq)
    s_c = [_dot_nt(q[g], kcmp_ref[g]).reshape(NSA_REP, tq, ncmp) for g in groups]
    s_w = [_dot_nt(q[g], kw_ref[0, g, pl.ds(w0, span), :]) for g in groups]

    t_c = s0 + lax.broadcasted_iota(I32, (1, tq, ncmp), 1)
    n_c = lax.broadcasted_iota(I32, (1, tq, ncmp), 2)
    mask_c = (n_c * CMP_STRIDE + (CMP_LEN - 1)) <= t_c
    maskf_c = jnp.where(mask_c, 1.0, 0.0)
    o_cmp, imp = [], []
    for g in groups:
        s_m = jnp.where(mask_c, s_c[g], NEG)
        p_c = jnp.exp(s_m - jnp.max(s_m, axis=-1, keepdims=True)) * maskf_c
        l_c = jnp.sum(p_c, axis=-1, keepdims=True)
        p_c = p_c * (1.0 / jnp.maximum(l_c, 1e-30))
        o_cmp.append(_dot(p_c.reshape(rows, ncmp).astype(MXU_DTYPE), vcmp_ref[g]))
        p_sum = jnp.sum(p_c, axis=0)
        p_hi = p_sum.astype(MXU_DTYPE)
        p_lo = (p_sum - p_hi.astype(F32)).astype(MXU_DTYPE)
        imp.append(_dot_nt(ovlt_ref[...], p_hi) + _dot_nt(ovlt_ref[...], p_lo))

    kpos = w0 + lax.broadcasted_iota(I32, (tq, span), 1)
    tt = s0 + lax.broadcasted_iota(I32, (tq, span), 0)
    bias_w = jnp.where(kpos <= tt, jnp.where(kpos > tt - WIN, 0.0, MASK_BIAS), MASK_BIAS)
    o_win = []
    for g in groups:
        _, p_w = _softmax_probs(s_w[g], bias_w, jnp.full((rows, 1), M_INIT, F32))
        o_win.append(_dot(p_w.astype(MXU_DTYPE), vw_ref[0, g, pl.ds(w0, span), :]))

    j_idx = lax.broadcasted_iota(I32, (nslc, tq), 0)
    t_s = s0 + lax.broadcasted_iota(I32, (nslc, tq), 1)
    cur = lax.shift_right_logical(t_s, int(np.log2(SLC_LEN)))
    blk_bias = []
    for g in groups:
        sc = jnp.where(j_idx == 0, jnp.inf,
                       jnp.where(j_idx == cur, jnp.inf, jnp.where(j_idx == cur - 1, jnp.inf, imp[g])))
        sc = jnp.where(j_idx <= cur, sc, -jnp.inf)
        rank = jnp.zeros((nslc, tq), I32)
        for i in range(nslc):
            ci = sc[i:i + 1, :]
            rank = rank + jnp.where(ci > sc, 1, jnp.where(ci == sc, jnp.where(j_idx > i, 1, 0), 0))
        bb = jnp.where(rank < nsel, jnp.where(j_idx <= cur, 0.0, MASK_BIAS), MASK_BIAS)
        blk_bias.append(bb.T.astype(MXU_DTYPE))

    def sel_body(kt, carry):
        off = pl.multiple_of(kt * tk, tk)
        kpos = off + lax.broadcasted_iota(I32, (tq, tk), 1)
        tt = s0 + lax.broadcasted_iota(I32, (tq, tk), 0)
        out = []
        for g in groups:
            m, acc = carry[g]
            k = ksl_ref[0, g, pl.ds(off, tk), :]
            v1 = vsl_ref[0, g, pl.ds(off, tk), :]
            bias = jnp.where(kpos <= tt, _dot(blk_bias[g], e_ref[kt]), MASK_BIAS)
            out.append(_softmax_step(_dot_nt(q[g], k), bias, m, acc, v1))
        return tuple(out)

    init = tuple((jnp.full((rows, 1), M_INIT, F32), jnp.zeros((rows, LANES), F32)) for _ in groups)
    sel = lax.fori_loop(0, s0 // tk + 1, sel_body, init)

    gt = gate_ref[0]
    outs = []
    for g in groups:
        for r in range(NSA_REP):
            sl = slice(r * tq, (r + 1) * tq)
            lane0 = HEAD_DIM + NSA_REP * 3 * g + 3 * r
            acc_s, acc_w = sel[g][1][sl], o_win[g][sl]
            w_s = gt * (1.0 / acc_s)
            w_w = gt * (1.0 / acc_w)
            outs.append(gt[:, lane0:lane0 + 1] * o_cmp[g][sl]
                        + w_s[:, lane0 + 1:lane0 + 2] * acc_s[:, :HEAD_DIM]
                        + w_w[:, lane0 + 2:lane0 + 3] * acc_w[:, :HEAD_DIM])
    o_ref[0] = jnp.concatenate(outs, axis=-1).astype(o_ref.dtype)


def _nsa(qn, kc_rows, vc_rows, ksl, vsl, kw, vw, gates, pe_k, pe_v, w1k, w2k, w1v, w2v, tk):
    B, _, T, hd = qn.shape
    tq = TQ
    ncmp = T // CMP_STRIDE
    nslc = T // SLC_LEN
    key = np.arange(T)
    e = (key[None, :] // SLC_LEN == np.arange(nslc)[:, None]).astype(np.float32)
    e = jnp.asarray(e.reshape(nslc, T // tk, tk).transpose(1, 0, 2), MXU_DTYPE)
    cmp_start = np.arange(ncmp) * CMP_STRIDE
    cmp_end = cmp_start + CMP_LEN - 1
    slc_start = np.arange(nslc) * SLC_LEN
    slc_end = slc_start + SLC_LEN - 1
    ovl = ((cmp_start[:, None] <= slc_end[None, :]) & (cmp_end[:, None] >= slc_start[None, :]))
    ovl[ncmp - 1, :] = False
    ovl_t = jnp.asarray(ovl.T.astype(np.float32), MXU_DTYPE)

    def both(shape):
        return pl.BlockSpec((1, NSA_KV) + shape, lambda b, i: (b, 0) + (0,) * len(shape))

    def whole(a):
        return pl.BlockSpec(a.shape, lambda b, i: (0,) * a.ndim)

    consts = (pe_k, pe_v, w1k, w2k, w1v, w2v, e, ovl_t)
    return pl.pallas_call(
        functools.partial(_nsa_kernel, tk=tk, seq=T),
        out_shape=jax.ShapeDtypeStruct((B, T, NSA_HEADS * hd), MXU_DTYPE),
        grid=(B, T // tq),
        in_specs=[pl.BlockSpec((1, NSA_HEADS, tq, hd), lambda b, i: (b, 0, i, 0)),
                  both((ncmp, CMP_STRIDE * hd)), both((ncmp, CMP_STRIDE * hd)),
                  both((T, hd)), both((T, LANES)), both((T, hd)), both((T, LANES)),
                  pl.BlockSpec((1, tq, LANES), lambda b, i: (b, i, 0))]
                 + [whole(a) for a in consts],
        out_specs=pl.BlockSpec((1, tq, NSA_HEADS * hd), lambda b, i: (b, i, 0)),
        scratch_shapes=[pltpu.VMEM((NSA_KV, ncmp, hd), MXU_DTYPE),
                        pltpu.VMEM((NSA_KV, ncmp, hd), MXU_DTYPE)],
        compiler_params=pltpu.CompilerParams(
            dimension_semantics=("parallel", "arbitrary"), vmem_limit_bytes=VMEM_LIMIT),
        name="nsa",
    )(qn, kc_rows, vc_rows, ksl, vsl, kw, vw, gates, *consts)


def _bit_transpose32(rows):
    a = list(rows)
    j, m = 16, 0x0000FFFF
    while j:
        for k in range(WORD):
            if k & j == 0:
                t = (a[k] ^ lax.shift_right_logical(a[k + j], np.int32(j))) & np.int32(m)
                a[k] = a[k] ^ t
                a[k + j] = a[k + j] ^ jnp.left_shift(t, np.int32(j))
        j >>= 1
        m ^= (m << j) & 0xFFFFFFFF
    return a


def _dsa_kernel(qd_ref, qi_ref, wi_ref, kd_ref, vd_ref, ki_ref, tri_ref, o_ref,
                key_ref, plane_ref, bias_ref, *, ck, ktop):
    tq = TQ
    s0 = pl.program_id(1) * tq
    n_kc = s0 // ck + 1
    hd = HEAD_DIM
    rows = DSA_HEADS * tq
    gpc = ck // (WORD * SUBLANES)
    n_groups = plane_ref.shape[1]

    w = wi_ref[0]
    q_idx = qi_ref[0].reshape(IDX_HEADS * tq, IDX_DIM)
    krow = lax.broadcasted_iota(I32, (ck, tq), 0)
    t_q = s0 + lax.broadcasted_iota(I32, (ck, tq), 1)

    def score_body(c, carry):
        off = pl.multiple_of(c * ck, ck)
        lg = _dot_nt(ki_ref[0, pl.ds(off, ck), :], q_idx)
        sc = w[0:1, :] * jnp.maximum(lg[:, :tq], 0.0)
        for h in range(1, IDX_HEADS):
            sc = sc + w[h:h + 1, :] * jnp.maximum(lg[:, h * tq:(h + 1) * tq], 0.0)
        sc = jnp.where(sc == 0.0, 0.0, sc)
        bits = pltpu.bitcast(sc, I32)
        key = jnp.where(bits < 0, bits ^ np.int32(0x7FFFFFFF), bits)
        key = jnp.where(off + krow <= t_q, key, INT_MIN)
        key_ref[c] = key
        for g in range(gpc):
            slabs = [key[(g * WORD + j) * SUBLANES:(g * WORD + j + 1) * SUBLANES, :] ^ INT_MIN
                     for j in range(WORD)]
            for i, plane in enumerate(_bit_transpose32(slabs)):
                plane_ref[i, c * gpc + g] = plane
        return carry

    lax.fori_loop(0, n_kc, score_body, 0)

    def clear_body(c, carry):
        plane_ref[:, pl.ds(c * gpc, gpc)] = jnp.zeros((WORD, gpc, SUBLANES, tq), I32)
        return carry

    lax.fori_loop(n_kc, n_groups // gpc, clear_body, 0)

    def bit_body(i, carry):
        cand, need, thr_u = carry
        ones = cand & plane_ref[i]
        cnt = jnp.sum(jnp.sum(lax.population_count(ones), axis=0), axis=0, keepdims=True)
        take = cnt >= need
        cand = jnp.where(take, ones, cand ^ ones)
        need = jnp.where(take, need, need - cnt)
        thr_u = thr_u | jnp.where(take, jnp.left_shift(np.int32(1), np.int32(WORD - 1) - i), 0)
        return cand, need, thr_u

    cand, need, thr_u = lax.fori_loop(
        0, WORD, bit_body,
        (jnp.full((n_groups, SUBLANES, tq), -1, I32), jnp.full((1, tq), ktop, I32),
         jnp.zeros((1, tq), I32)))
    thr = thr_u ^ INT_MIN
    n_eq = jnp.sum(jnp.sum(lax.population_count(cand), axis=0), axis=0, keepdims=True)
    need_eq = need.astype(F32)
    exact_cut = jnp.max(jnp.where(thr_u == 0, 1, jnp.abs(n_eq - need))) == 0

    def bias_body(c, eq_seen):
        off = pl.multiple_of(c * ck, ck)
        key = key_ref[c]

        def cut_bias():
            return jnp.where(key >= thr, 0.0, MASK_BIAS), eq_seen

        def tie_bias():
            eq_f = jnp.where(key == thr, 1.0, 0.0)
            before = _dot(tri_ref[...], eq_f.astype(MXU_DTYPE)) + eq_seen
            take = jnp.where(key > thr, 0.0,
                             jnp.where(key == thr, jnp.where(before < need_eq, 0.0, MASK_BIAS), MASK_BIAS))
            return (jnp.where(off + krow <= t_q, take, MASK_BIAS),
                    eq_seen + jnp.sum(eq_f, axis=0, keepdims=True))

        bias_t, eq_seen = lax.cond(exact_cut, cut_bias, tie_bias)
        bias_ref[c] = bias_t.T.astype(bias_ref.dtype)
        return eq_seen

    lax.fori_loop(0, n_kc, bias_body, jnp.zeros((1, tq), F32))

    q = qd_ref[0].reshape(rows, hd)

    def att_body(c, carry):
        m, acc = carry
        off = pl.multiple_of(c * ck, ck)
        k = kd_ref[0, pl.ds(off, ck), :]
        v1 = vd_ref[0, pl.ds(off, ck), :]
        return _softmax_step(_dot_nt(q, k), bias_ref[c], m, acc, v1)

    init = (jnp.full((rows, 1), M_INIT, F32), jnp.zeros((rows, LANES), F32))
    _, acc = lax.fori_loop(0, n_kc, att_body, init)
    o = _normalize(acc)
    o_ref[0] = jnp.concatenate([o[h * tq:(h + 1) * tq] for h in range(DSA_HEADS)],
                               axis=-1).astype(o_ref.dtype)


def _dsa(qd, qi, wi_t, kd, vd, ki, ck):
    B, _, T, hd = qd.shape
    tq = TQ
    ktop = min(DSA_TOPK, T // 4)
    tri = jnp.asarray(np.tril(np.ones((ck, ck), np.float32), -1), MXU_DTYPE)
    kv_spec = pl.BlockSpec((1, T, hd), lambda b, i: (b, 0, 0))
    v_spec = pl.BlockSpec((1, T, LANES), lambda b, i: (b, 0, 0))
    return pl.pallas_call(
        functools.partial(_dsa_kernel, ck=ck, ktop=ktop),
        out_shape=jax.ShapeDtypeStruct((B, T, DSA_HEADS * hd), MXU_DTYPE),
        grid=(B, T // tq),
        in_specs=[pl.BlockSpec((1, DSA_HEADS, tq, hd), lambda b, i: (b, 0, i, 0)),
                  pl.BlockSpec((1, IDX_HEADS, tq, IDX_DIM), lambda b, i: (b, 0, i, 0)),
                  pl.BlockSpec((1, IDX_HEADS, tq), lambda b, i: (b, 0, i)),
                  kv_spec, v_spec, kv_spec,
                  pl.BlockSpec((ck, ck), lambda b, i: (0, 0))],
        out_specs=pl.BlockSpec((1, tq, DSA_HEADS * hd), lambda b, i: (b, i, 0)),
        scratch_shapes=[pltpu.VMEM((T // ck, ck, tq), I32),
                        pltpu.VMEM((WORD, T // (WORD * SUBLANES), SUBLANES, tq), I32),
                        pltpu.VMEM((T // ck, tq, ck), SOFTMAX_DTYPE)],
        compiler_params=pltpu.CompilerParams(
            dimension_semantics=("parallel", "parallel"), vmem_limit_bytes=VMEM_LIMIT),
        name="dsa",
    )(qd, qi, wi_t, kd, vd, ki, tri)


def _ffn_kernel(on_ref, od_ref, x_ref, mod_ref, gpm_ref, gpf_ref, gpo_ref,
                woa_ref, wob_ref, wup_ref, wdn_ref, out_ref, *, fc):
    o = _dot(on_ref[0], woa_ref[...]) + _dot(od_ref[0], wob_ref[...])
    x1 = x_ref[0] + mod_ref[0, 2:3, :] * _rms(o, gpm_ref[...])
    h = _rms(x1, gpf_ref[...]) * (1.0 + mod_ref[0, 4:5, :]) + mod_ref[0, 3:4, :]
    hb = h.astype(MXU_DTYPE)
    d_ff = wup_ref.shape[1]
    y = jnp.zeros(x1.shape, F32)
    for c in range(d_ff // fc):
        u = jnp.maximum(_dot(hb, wup_ref[:, c * fc:(c + 1) * fc]), 0.0)
        y = y + _dot((u * u).astype(MXU_DTYPE), wdn_ref[c * fc:(c + 1) * fc, :])
    out_ref[0] = x1 + mod_ref[0, 5:6, :] * _rms(y, gpo_ref[...])


def _ffn(o_nsa, o_dsa, x, mod, g_post_mix, g_pre_ffn, g_post_ffn, w_out, w_up, w_down, tm, fc):
    B, T, D = x.shape
    dn = o_nsa.shape[-1]

    def tok(w):
        return pl.BlockSpec((1, tm, w), lambda b, i: (b, i, 0))

    def whole(a):
        return pl.BlockSpec(a.shape, lambda b, i: (0,) * a.ndim, pipeline_mode=pl.Buffered(1))

    woa, wob = w_out[:dn], w_out[dn:]
    vec = pl.BlockSpec((1, D), lambda b, i: (0, 0))
    return pl.pallas_call(
        functools.partial(_ffn_kernel, fc=fc),
        out_shape=jax.ShapeDtypeStruct((B, T, D), F32),
        grid=(B, T // tm),
        in_specs=[tok(dn), tok(o_dsa.shape[-1]), tok(D),
                  pl.BlockSpec((1, 6, D), lambda b, i: (b, 0, 0)), vec, vec, vec,
                  whole(woa), whole(wob), whole(w_up), whole(w_down)],
        out_specs=tok(D),
        compiler_params=pltpu.CompilerParams(
            dimension_semantics=("parallel", "parallel"), vmem_limit_bytes=VMEM_LIMIT),
        name="ffn",
    )(o_nsa, o_dsa, x, mod, g_post_mix, g_pre_ffn, g_post_ffn, woa, wob, w_up, w_down)


def _permute_w_in(w):
    scale = {"q_n": HEAD_DIM ** -0.5, "q_d": HEAD_DIM ** -0.5}
    cols = [w[:, _SEG[n][0]:_SEG[n][1]] * scale.get(n, 1.0) for n in _ROPE_ORDER + _PLAIN_ORDER]
    used = sum(c.shape[1] for c in cols)
    cols.append(jnp.zeros((w.shape[0], D_PROJ - used), w.dtype))
    return jnp.concatenate(cols, axis=1).astype(MXU_DTYPE)


def kernel(x, c, positions, w_ada, b_ada, g_pre_mix, g_post_mix, g_pre_ffn, g_post_ffn,
           w_in, cmp_pe_k, cmp_pe_v, cmp_w1_k, cmp_w2_k, cmp_w1_v, cmp_w2_v,
           w_out, w_up, w_down):
    B, T, D = x.shape
    depth = w_ada.shape[0]
    tm = min(TM, T)
    tk = min(TK, T)
    half = HEAD_DIM // 2
    inv = ROPE_THETA ** (-jnp.arange(half, dtype=F32) / half)
    inv_tile = jnp.tile(inv, LANES // half).reshape(1, LANES)
    pos_f = positions.astype(F32).reshape(B, T, 1)

    for l in range(depth):
        mod = _ada(c, w_ada[l], b_ada[l]).reshape(B, 6, D)
        (qn, kc, vc, ksl, vsl, kw, vw, gates, qd, kd, vd, qi, ki, wi) = _in_proj(
            x, mod, g_pre_mix[l].reshape(1, D), _permute_w_in(w_in[l]), pos_f, inv_tile, tm)
        o_nsa = _nsa(qn, kc, vc, ksl, vsl, kw, vw, gates,
                     cmp_pe_k[l].reshape(1, -1), cmp_pe_v[l].reshape(1, -1),
                     cmp_w1_k[l].astype(MXU_DTYPE), cmp_w2_k[l].astype(MXU_DTYPE),
                     cmp_w1_v[l].astype(MXU_DTYPE), cmp_w2_v[l].astype(MXU_DTYPE), tk=tk)
        o_dsa = _dsa(qd, qi, wi, kd, vd, ki, ck=tk)
        x = _ffn(o_nsa, o_dsa, x, mod, g_post_mix[l].reshape(1, D), g_pre_ffn[l].reshape(1, D),
                 g_post_ffn[l].reshape(1, D), w_out[l].astype(MXU_DTYPE),
                 w_up[l].astype(MXU_DTYPE), w_down[l].astype(MXU_DTYPE), tm=tm, fc=FFN_CHUNK)
    return x
```

```python
import functools

import numpy as np
import jax
import jax.numpy as jnp
from jax import lax
from jax.experimental import pallas as pl
from jax.experimental.pallas import tpu as pltpu

HEAD_DIM = 64
NSA_HEADS = 8
NSA_KV = 2
NSA_REP = NSA_HEADS // NSA_KV
DSA_HEADS = 8
IDX_HEADS = 4
IDX_DIM = 64
CMP_LEN = 32
CMP_STRIDE = 16
SLC_LEN = 64
SLC_TOPN = 16
WIN = 512
DSA_TOPK = 256
ROPE_THETA = 10000.0
EPS = 1e-6
NEG = -1e30
MASK_BIAS = -2e30
M_INIT = -1e30

LANES = 128
SUBLANES = 8
MXU_DTYPE = jnp.bfloat16
SOFTMAX_DTYPE = jnp.bfloat16
VMEM_LIMIT = 56 * 1024 * 1024

TQ = 2 * LANES
TM = 512
TK = 512
FFN_CHUNK = 1024
WORD = 32
F32 = jnp.float32
I32 = jnp.int32
INT_MIN = np.int32(-2 ** 31)

_SEG = dict(q_n=(0, 512), kc=(512, 640), vc=(640, 768), ksl=(768, 896), vsl=(896, 1024),
            kw=(1024, 1152), vw=(1152, 1280), gl=(1280, 1304), q_d=(1304, 1816),
            k_d=(1816, 1880), v_d=(1880, 1944), qi=(1944, 2200), ki=(2200, 2264), wi=(2264, 2268))
_ROPE_ORDER = ("q_n", "kc", "ksl", "kw", "q_d", "qi", "k_d", "ki")
_PLAIN_ORDER = ("vc", "vsl", "vw", "v_d", "gl", "wi")
D_PROJ = 18 * LANES


def _dot(a, b):
    return jnp.dot(a, b, preferred_element_type=F32)


def _dot_nt(a, b):
    return lax.dot_general(a, b, (((1,), (1,)), ((), ())), preferred_element_type=F32)


def _rms(x, g):
    return x * lax.rsqrt(jnp.mean(x * x, axis=-1, keepdims=True) + EPS) * g


def _ada_kernel(c_ref, w_ref, b_ref, o_ref):
    o_ref[...] = _dot(c_ref[...].astype(MXU_DTYPE), w_ref[...].astype(MXU_DTYPE)) + b_ref[...]


def _ada(c, w, b):
    B, D = c.shape
    n = w.shape[1] // D
    return pl.pallas_call(
        _ada_kernel,
        out_shape=jax.ShapeDtypeStruct((B, n * D), F32),
        grid=(n,),
        in_specs=[pl.BlockSpec((B, D), lambda j: (0, 0)),
                  pl.BlockSpec((D, D), lambda j: (0, j)),
                  pl.BlockSpec((1, D), lambda j: (0, j))],
        out_specs=pl.BlockSpec((B, D), lambda j: (0, j)),
        name="ada",
    )(c, w, b.reshape(1, -1))


def _inproj_kernel(x_ref, mod_ref, g_ref, w_ref, pos_ref, inv_ref,
                   qn_ref, kc_ref, vc_ref, ksl_ref, vsl_ref, kw_ref, vw_ref, gate_ref,
                   qd_ref, kd_ref, vd_ref, qi_ref, ki_ref, wi_ref, rows_ref):
    x = x_ref[0]
    tm = x.shape[0]
    h = _rms(x, g_ref[...] * (1.0 + mod_ref[0, 1:2, :])) + mod_ref[0, 0:1, :]
    proj = _dot(h.astype(MXU_DTYPE), w_ref[...])

    ang = pos_ref[0] * inv_ref[...]
    lane = lax.broadcasted_iota(I32, (tm, LANES), 1)
    first_half = (lane & (HEAD_DIM // 2)) == 0
    cosv = jnp.cos(ang)
    sinv = jnp.sin(ang)
    sin_signed = jnp.where(first_half, -sinv, sinv)

    def rope(j):
        c = proj[:, j * LANES:(j + 1) * LANES]
        swapped = jnp.where(first_half, pltpu.roll(c, LANES - HEAD_DIM // 2, 1),
                            pltpu.roll(c, HEAD_DIM // 2, 1))
        return c * cosv + swapped * sin_signed

    def plain(j):
        return proj[:, j * LANES:(j + 1) * LANES]

    dt = qn_ref.dtype
    for j in range(4):
        r = rope(j)
        qn_ref[0, 2 * j] = r[:, :HEAD_DIM].astype(dt)
        qn_ref[0, 2 * j + 1] = r[:, HEAD_DIM:].astype(dt)
    def put_rows(ref, c):
        rows_ref[...] = c
        for t in range(CMP_STRIDE):
            sub = rows_ref[pl.ds(t, tm // CMP_STRIDE, stride=CMP_STRIDE), :]
            ref[0, 0, :, t * HEAD_DIM:(t + 1) * HEAD_DIM] = sub[:, :HEAD_DIM].astype(dt)
            ref[0, 1, :, t * HEAD_DIM:(t + 1) * HEAD_DIM] = sub[:, HEAD_DIM:].astype(dt)

    put_rows(kc_ref, rope(4))
    for ref, j in ((ksl_ref, 5), (kw_ref, 6)):
        r = rope(j)
        ref[0, 0] = r[:, :HEAD_DIM].astype(dt)
        ref[0, 1] = r[:, HEAD_DIM:].astype(dt)
    for j in range(4):
        r = rope(7 + j)
        qd_ref[0, 2 * j] = r[:, :HEAD_DIM].astype(dt)
        qd_ref[0, 2 * j + 1] = r[:, HEAD_DIM:].astype(dt)
    for j in range(2):
        r = rope(11 + j)
        qi_ref[0, 2 * j] = r[:, :HEAD_DIM].astype(dt)
        qi_ref[0, 2 * j + 1] = r[:, HEAD_DIM:].astype(dt)
    r = rope(13)
    kd_ref[0] = r[:, :HEAD_DIM].astype(dt)
    ki_ref[0] = r[:, HEAD_DIM:].astype(dt)
    put_rows(vc_ref, plain(14))
    def with_ones(c):
        return jnp.where(lane < HEAD_DIM, c, 1.0).astype(dt)

    for ref, j in ((vsl_ref, 15), (vw_ref, 16)):
        v = plain(j)
        ref[0, 0] = with_ones(v)
        ref[0, 1] = with_ones(pltpu.roll(v, HEAD_DIM, 1))
    last = plain(17)
    vd_ref[0] = with_ones(last)
    ng = NSA_REP * 3
    gate_ref[0] = jax.nn.sigmoid(last)
    idx_scale = (IDX_HEADS ** -0.5) * (IDX_DIM ** -0.5)
    wi0 = HEAD_DIM + 2 * ng
    wi_ref[0] = last.T[wi0:wi0 + IDX_HEADS, :] * idx_scale


def _in_proj(x, mod, g_pre, w_perm, pos_f, inv_tile, tm):
    B, T, D = x.shape
    dt = MXU_DTYPE
    hd = HEAD_DIM

    def heads(n, w=hd):
        return (jax.ShapeDtypeStruct((B, n, T, w), dt),
                pl.BlockSpec((1, n, tm, w), lambda b, i: (b, 0, i, 0)))

    def flat(w, d=dt):
        return (jax.ShapeDtypeStruct((B, T, w), d),
                pl.BlockSpec((1, tm, w), lambda b, i: (b, i, 0)))

    row_w = CMP_STRIDE * hd
    row_view = (jax.ShapeDtypeStruct((B, NSA_KV, T // CMP_STRIDE, row_w), dt),
                pl.BlockSpec((1, NSA_KV, tm // CMP_STRIDE, row_w), lambda b, i: (b, 0, i, 0)))
    outs = [heads(NSA_HEADS),
            row_view, row_view,
            heads(NSA_KV), heads(NSA_KV, LANES),
            heads(NSA_KV), heads(NSA_KV, LANES),
            flat(LANES, F32),
            heads(DSA_HEADS),
            flat(hd), flat(LANES),
            heads(IDX_HEADS),
            flat(hd),
            (jax.ShapeDtypeStruct((B, IDX_HEADS, T), F32),
             pl.BlockSpec((1, IDX_HEADS, tm), lambda b, i: (b, 0, i)))]
    return pl.pallas_call(
        _inproj_kernel,
        out_shape=[o[0] for o in outs],
        grid=(B, T // tm),
        in_specs=[pl.BlockSpec((1, tm, D), lambda b, i: (b, i, 0)),
                  pl.BlockSpec((1, 6, D), lambda b, i: (b, 0, 0)),
                  pl.BlockSpec((1, D), lambda b, i: (0, 0)),
                  pl.BlockSpec((D, D_PROJ), lambda b, i: (0, 0)),
                  pl.BlockSpec((1, tm, 1), lambda b, i: (b, i, 0)),
                  pl.BlockSpec((1, LANES), lambda b, i: (0, 0))],
        out_specs=[o[1] for o in outs],
        scratch_shapes=[pltpu.VMEM((tm, LANES), F32)],
        compiler_params=pltpu.CompilerParams(
            dimension_semantics=("parallel", "parallel"), vmem_limit_bytes=VMEM_LIMIT),
        name="in_proj",
    )(x, mod, g_pre, w_perm, pos_f, inv_tile)


def _softmax_probs(s, bias, m_prev):
    tq, n = bias.shape
    sb = (s.astype(SOFTMAX_DTYPE).reshape(-1, tq, n) + bias.astype(SOFTMAX_DTYPE)[None]).reshape(s.shape)
    m_new = jnp.maximum(m_prev, jnp.max(sb, axis=-1, keepdims=True).astype(F32))
    return m_new, jnp.exp(sb - m_new.astype(SOFTMAX_DTYPE))


def _softmax_step(s, bias, m_prev, acc_prev, v1):
    m_new, p = _softmax_probs(s, bias, m_prev)
    acc_new = jnp.exp(m_prev - m_new) * acc_prev + _dot(p.astype(v1.dtype), v1)
    return m_new, acc_new


def _normalize(acc):
    return acc[:, :HEAD_DIM] * (1.0 / acc[:, HEAD_DIM:HEAD_DIM + 1])


def _nsa_kernel(q_ref, kc_ref, vc_ref, ksl_ref, vsl_ref, kw_ref, vw_ref, gate_ref,
                pek_ref, pev_ref, w1k_ref, w2k_ref, w1v_ref, w2v_ref, e_ref, ovlt_ref,
                o_ref, kcmp_ref, vcmp_ref, *, tk, seq):
    tq = TQ
    qi = pl.program_id(1)
    s0 = qi * tq
    ncmp = seq // CMP_STRIDE
    nslc = seq // SLC_LEN
    nsel = min(SLC_TOPN, nslc)
    rows = NSA_REP * tq
    hd = HEAD_DIM
    groups = range(NSA_KV)

    @pl.when(qi == 0)
    def _():
        half = CMP_STRIDE * hd

        def compress(r, pe_ref, w1_ref, w2_ref):
            a = _dot(r, w1_ref[:half, :])
            b = _dot(r, w1_ref[half:, :])
            pe = jnp.broadcast_to(pe_ref[...], (SUBLANES, CMP_LEN * hd)).astype(MXU_DTYPE)
            bias = _dot(pe, w1_ref[...])[0:1, :]
            pre = a + pltpu.roll(b, ncmp - 1, 0) + bias
            return _dot(jax.nn.gelu(pre).astype(MXU_DTYPE), w2_ref[...])

        for g in groups:
            kcmp_ref[g] = compress(kc_ref[0, g], pek_ref, w1k_ref, w2k_ref).astype(kcmp_ref.dtype)
            vcmp_ref[g] = compress(vc_ref[0, g], pev_ref, w1v_ref, w2v_ref).astype(vcmp_ref.dtype)

    q = [q_ref[0, g * NSA_REP:(g + 1) * NSA_REP].reshape(rows, hd) for g in groups]

    hq = tq // 2
    span = WIN + hq
    halves = range(2)
    w0 = [pl.multiple_of(jnp.maximum(s0 + hh * hq - WIN, 0), hq) for hh in halves]
    s_c = [_dot_nt(q[g], kcmp_ref[g]).reshape(NSA_REP, tq, ncmp) for g in groups]
    s_w = [[_dot_nt(q_ref[0, g * NSA_REP:(g + 1) * NSA_REP, hh * hq:(hh + 1) * hq].reshape(NSA_REP * hq, hd),
                    kw_ref[0, g, pl.ds(w0[hh], span), :]) for hh in halves] for g in groups]

    t_c = s0 + lax.broadcasted_iota(I32, (1, tq, ncmp), 1)
    n_c = lax.broadcasted_iota(I32, (1, tq, ncmp), 2)
    mask_c = (n_c * CMP_STRIDE + (CMP_LEN - 1)) <= t_c
    maskf_c = jnp.where(mask_c, 1.0, 0.0)
    o_cmp, imp = [], []
    for g in groups:
        s_m = jnp.where(mask_c, s_c[g], NEG)
        p_c = jnp.exp(s_m - jnp.max(s_m, axis=-1, keepdims=True)) * maskf_c
        l_c = jnp.sum(p_c, axis=-1, keepdims=True)
        p_c = p_c * (1.0 / jnp.maximum(l_c, 1e-30))
        o_cmp.append(_dot(p_c.reshape(rows, ncmp).astype(MXU_DTYPE), vcmp_ref[g]))
        p_sum = jnp.sum(p_c, axis=0)
        p_hi = p_sum.astype(MXU_DTYPE)
        p_lo = (p_sum - p_hi.astype(F32)).astype(MXU_DTYPE)
        imp.append(_dot_nt(ovlt_ref[...], p_hi) + _dot_nt(ovlt_ref[...], p_lo))

    bias_w = []
    for hh in halves:
        kpos = w0[hh] + lax.broadcasted_iota(I32, (hq, span), 1)
        tt = s0 + hh * hq + lax.broadcasted_iota(I32, (hq, span), 0)
        bias_w.append(jnp.where(kpos <= tt, jnp.where(kpos > tt - WIN, 0.0, MASK_BIAS), MASK_BIAS))
    o_win = []
    for g in groups:
        acc_h = []
        for hh in halves:
            _, p_w = _softmax_probs(s_w[g][hh], bias_w[hh], jnp.full((NSA_REP * hq, 1), M_INIT, F32))
            acc = _dot(p_w.astype(MXU_DTYPE), vw_ref[0, g, pl.ds(w0[hh], span), :])
            acc_h.append(acc.reshape(NSA_REP, hq, LANES))
        o_win.append(jnp.concatenate(acc_h, axis=1).reshape(rows, LANES))

    j_idx = lax.broadcasted_iota(I32, (nslc, tq), 0)
    t_s = s0 + lax.broadcasted_iota(I32, (nslc, tq), 1)
    cur = lax.shift_right_logical(t_s, int(np.log2(SLC_LEN)))
    blk_bias = []
    for g in groups:
        sc = jnp.where(j_idx == 0, jnp.inf,
                       jnp.where(j_idx == cur, jnp.inf, jnp.where(j_idx == cur - 1, jnp.inf, imp[g])))
        sc = jnp.where(j_idx <= cur, sc, -jnp.inf)
        rank = jnp.zeros((nslc, tq), I32)
        for i in range(nslc):
            ci = sc[i:i + 1, :]
            rank = rank + jnp.where(ci > sc, 1, jnp.where(ci == sc, jnp.where(j_idx > i, 1, 0), 0))
        bb = jnp.where(rank < nsel, jnp.where(j_idx <= cur, 0.0, MASK_BIAS), MASK_BIAS)
        blk_bias.append(bb.T.astype(MXU_DTYPE))

    def sel_body(kt, carry):
        off = pl.multiple_of(kt * tk, tk)
        kpos = off + lax.broadcasted_iota(I32, (tq, tk), 1)
        tt = s0 + lax.broadcasted_iota(I32, (tq, tk), 0)
        out = []
        for g in groups:
            m, acc = carry[g]
            k = ksl_ref[0, g, pl.ds(off, tk), :]
            v1 = vsl_ref[0, g, pl.ds(off, tk), :]
            bias = jnp.where(kpos <= tt, _dot(blk_bias[g], e_ref[kt]), MASK_BIAS)
            out.append(_softmax_step(_dot_nt(q[g], k), bias, m, acc, v1))
        return tuple(out)

    init = tuple((jnp.full((rows, 1), M_INIT, F32), jnp.zeros((rows, LANES), F32)) for _ in groups)
    sel = lax.fori_loop(0, s0 // tk + 1, sel_body, init)

    gt = gate_ref[0]
    outs = []
    for g in groups:
        for r in range(NSA_REP):
            sl = slice(r * tq, (r + 1) * tq)
            lane0 = HEAD_DIM + NSA_REP * 3 * g + 3 * r
            acc_s, acc_w = sel[g][1][sl], o_win[g][sl]
            w_s = gt * (1.0 / acc_s)
            w_w = gt * (1.0 / acc_w)
            outs.append(gt[:, lane0:lane0 + 1] * o_cmp[g][sl]
                        + w_s[:, lane0 + 1:lane0 + 2] * acc_s[:, :HEAD_DIM]
                        + w_w[:, lane0 + 2:lane0 + 3] * acc_w[:, :HEAD_DIM])
    o_ref[0] = jnp.concatenate(outs, axis=-1).astype(o_ref.dtype)


def _nsa(qn, kc_rows, vc_rows, ksl, vsl, kw, vw, gates, pe_k, pe_v, w1k, w2k, w1v, w2v, tk):
    B, _, T, hd = qn.shape
    tq = TQ
    ncmp = T // CMP_STRIDE
    nslc = T // SLC_LEN
    key = np.arange(T)
    e = (key[None, :] // SLC_LEN == np.arange(nslc)[:, None]).astype(np.float32)
    e = jnp.asarray(e.reshape(nslc, T // tk, tk).transpose(1, 0, 2), MXU_DTYPE)
    cmp_start = np.arange(ncmp) * CMP_STRIDE
    cmp_end = cmp_start + CMP_LEN - 1
    slc_start = np.arange(nslc) * SLC_LEN
    slc_end = slc_start + SLC_LEN - 1
    ovl = ((cmp_start[:, None] <= slc_end[None, :]) & (cmp_end[:, None] >= slc_start[None, :]))
    ovl[ncmp - 1, :] = False
    ovl_t = jnp.asarray(ovl.T.astype(np.float32), MXU_DTYPE)

    def both(shape):
        return pl.BlockSpec((1, NSA_KV) + shape, lambda b, i: (b, 0) + (0,) * len(shape))

    def whole(a):
        return pl.BlockSpec(a.shape, lambda b, i: (0,) * a.ndim)

    consts = (pe_k, pe_v, w1k, w2k, w1v, w2v, e, ovl_t)
    return pl.pallas_call(
        functools.partial(_nsa_kernel, tk=tk, seq=T),
        out_shape=jax.ShapeDtypeStruct((B, T, NSA_HEADS * hd), MXU_DTYPE),
        grid=(B, T // tq),
        in_specs=[pl.BlockSpec((1, NSA_HEADS, tq, hd), lambda b, i: (b, 0, i, 0)),
                  both((ncmp, CMP_STRIDE * hd)), both((ncmp, CMP_STRIDE * hd)),
                  both((T, hd)), both((T, LANES)), both((T, hd)), both((T, LANES)),
                  pl.BlockSpec((1, tq, LANES), lambda b, i: (b, i, 0))]
                 + [whole(a) for a in consts],
        out_specs=pl.BlockSpec((1, tq, NSA_HEADS * hd), lambda b, i: (b, i, 0)),
        scratch_shapes=[pltpu.VMEM((NSA_KV, ncmp, hd), MXU_DTYPE),
                        pltpu.VMEM((NSA_KV, ncmp, hd), MXU_DTYPE)],
        compiler_params=pltpu.CompilerParams(
            dimension_semantics=("parallel", "arbitrary"), vmem_limit_bytes=VMEM_LIMIT),
        name="nsa",
    )(qn, kc_rows, vc_rows, ksl, vsl, kw, vw, gates, *consts)


def _bit_transpose32(rows):
    a = list(rows)
    j, m = 16, 0x0000FFFF
    while j:
        for k in range(WORD):
            if k & j == 0:
                t = (a[k] ^ lax.shift_right_logical(a[k + j], np.int32(j))) & np.int32(m)
                a[k] = a[k] ^ t
                a[k + j] = a[k + j] ^ jnp.left_shift(t, np.int32(j))
        j >>= 1
        m ^= (m << j) & 0xFFFFFFFF
    return a


def _dsa_kernel(qd_ref, qi_ref, wi_ref, kd_ref, vd_ref, ki_ref, tri_ref, o_ref,
                key_ref, plane_ref, *, ck, ktop):
    tq = TQ
    s0 = pl.program_id(1) * tq
    n_kc = s0 // ck + 1
    hd = HEAD_DIM
    rows = DSA_HEADS * tq
    gpc = ck // (WORD * SUBLANES)
    n_groups = plane_ref.shape[1]

    w = wi_ref[0]
    q_idx = qi_ref[0].reshape(IDX_HEADS * tq, IDX_DIM)
    krow = lax.broadcasted_iota(I32, (ck, tq), 0)
    t_q = s0 + lax.broadcasted_iota(I32, (ck, tq), 1)

    def score_body(c, carry):
        off = pl.multiple_of(c * ck, ck)
        lg = _dot_nt(ki_ref[0, pl.ds(off, ck), :], q_idx)
        sc = w[0:1, :] * jnp.maximum(lg[:, :tq], 0.0)
        for h in range(1, IDX_HEADS):
            sc = sc + w[h:h + 1, :] * jnp.maximum(lg[:, h * tq:(h + 1) * tq], 0.0)
        sc = jnp.where(sc == 0.0, 0.0, sc)
        bits = pltpu.bitcast(sc, I32)
        key = jnp.where(bits < 0, bits ^ np.int32(0x7FFFFFFF), bits)
        key = jnp.where(off + krow <= t_q, key, INT_MIN)
        key_ref[c] = key
        for g in range(gpc):
            slabs = [key[(g * WORD + j) * SUBLANES:(g * WORD + j + 1) * SUBLANES, :] ^ INT_MIN
                     for j in range(WORD)]
            for i, plane in enumerate(_bit_transpose32(slabs)):
                plane_ref[i, c * gpc + g] = plane
        return carry

    lax.fori_loop(0, n_kc, score_body, 0)

    def clear_body(c, carry):
        plane_ref[:, pl.ds(c * gpc, gpc)] = jnp.zeros((WORD, gpc, SUBLANES, tq), I32)
        return carry

    lax.fori_loop(n_kc, n_groups // gpc, clear_body, 0)

    def bit_body(i, carry):
        cand, need, thr_u = carry
        ones = cand & plane_ref[i]
        cnt = jnp.sum(jnp.sum(lax.population_count(ones), axis=0), axis=0, keepdims=True)
        take = cnt >= need
        cand = jnp.where(take, ones, cand ^ ones)
        need = jnp.where(take, need, need - cnt)
        thr_u = thr_u | jnp.where(take, jnp.left_shift(np.int32(1), np.int32(WORD - 1) - i), 0)
        return cand, need, thr_u

    cand, need, thr_u = lax.fori_loop(
        0, WORD, bit_body,
        (jnp.full((n_groups, SUBLANES, tq), -1, I32), jnp.full((1, tq), ktop, I32),
         jnp.zeros((1, tq), I32)))
    thr = thr_u ^ INT_MIN
    n_eq = jnp.sum(jnp.sum(lax.population_count(cand), axis=0), axis=0, keepdims=True)
    need_eq = need.astype(F32)
    exact_cut = jnp.max(jnp.where(thr_u == 0, 1, jnp.abs(n_eq - need))) == 0

    q = qd_ref[0].reshape(rows, hd)

    def att_body(c, carry):
        m, acc, eq_seen = carry
        off = pl.multiple_of(c * ck, ck)
        key = key_ref[c]

        def cut_bias():
            return jnp.where(key >= thr, 0.0, MASK_BIAS), eq_seen

        def tie_bias():
            eq_f = jnp.where(key == thr, 1.0, 0.0)
            before = _dot(tri_ref[...], eq_f.astype(MXU_DTYPE)) + eq_seen
            take = jnp.where(key > thr, 0.0,
                             jnp.where(key == thr, jnp.where(before < need_eq, 0.0, MASK_BIAS), MASK_BIAS))
            return (jnp.where(off + krow <= t_q, take, MASK_BIAS),
                    eq_seen + jnp.sum(eq_f, axis=0, keepdims=True))

        bias_t, eq_seen = lax.cond(exact_cut, cut_bias, tie_bias)
        bias = bias_t.T
        k = kd_ref[0, pl.ds(off, ck), :]
        v1 = vd_ref[0, pl.ds(off, ck), :]
        m, acc = _softmax_step(_dot_nt(q, k), bias, m, acc, v1)
        return m, acc, eq_seen

    init = (jnp.full((rows, 1), M_INIT, F32), jnp.zeros((rows, LANES), F32),
            jnp.zeros((1, tq), F32))
    _, acc, _ = lax.fori_loop(0, n_kc, att_body, init)
    o = _normalize(acc)
    o_ref[0] = jnp.concatenate([o[h * tq:(h + 1) * tq] for h in range(DSA_HEADS)],
                               axis=-1).astype(o_ref.dtype)


def _dsa(qd, qi, wi_t, kd, vd, ki, ck):
    B, _, T, hd = qd.shape
    tq = TQ
    ktop = min(DSA_TOPK, T // 4)
    tri = jnp.asarray(np.tril(np.ones((ck, ck), np.float32), -1), MXU_DTYPE)
    kv_spec = pl.BlockSpec((1, T, hd), lambda b, i: (b, 0, 0))
    v_spec = pl.BlockSpec((1, T, LANES), lambda b, i: (b, 0, 0))
    return pl.pallas_call(
        functools.partial(_dsa_kernel, ck=ck, ktop=ktop),
        out_shape=jax.ShapeDtypeStruct((B, T, DSA_HEADS * hd), MXU_DTYPE),
        grid=(B, T // tq),
        in_specs=[pl.BlockSpec((1, DSA_HEADS, tq, hd), lambda b, i: (b, 0, i, 0)),
                  pl.BlockSpec((1, IDX_HEADS, tq, IDX_DIM), lambda b, i: (b, 0, i, 0)),
                  pl.BlockSpec((1, IDX_HEADS, tq), lambda b, i: (b, 0, i)),
                  kv_spec, v_spec, kv_spec,
                  pl.BlockSpec((ck, ck), lambda b, i: (0, 0))],
        out_specs=pl.BlockSpec((1, tq, DSA_HEADS * hd), lambda b, i: (b, i, 0)),
        scratch_shapes=[pltpu.VMEM((T // ck, ck, tq), I32),
                        pltpu.VMEM((WORD, T // (WORD * SUBLANES), SUBLANES, tq), I32)],
        compiler_params=pltpu.CompilerParams(
            dimension_semantics=("parallel", "parallel"), vmem_limit_bytes=VMEM_LIMIT),
        name="dsa",
    )(qd, qi, wi_t, kd, vd, ki, tri)


def _ffn_kernel(on_ref, od_ref, x_ref, mod_ref, gpm_ref, gpf_ref, gpo_ref,
                woa_ref, wob_ref, wup_ref, wdn_ref, out_ref, *, fc):
    o = _dot(on_ref[0], woa_ref[...]) + _dot(od_ref[0], wob_ref[...])
    x1 = x_ref[0] + mod_ref[0, 2:3, :] * _rms(o, gpm_ref[...])
    h = _rms(x1, gpf_ref[...]) * (1.0 + mod_ref[0, 4:5, :]) + mod_ref[0, 3:4, :]
    hb = h.astype(MXU_DTYPE)
    d_ff = wup_ref.shape[1]
    y = jnp.zeros(x1.shape, F32)
    for c in range(d_ff // fc):
        u = jnp.maximum(_dot(hb, wup_ref[:, c * fc:(c + 1) * fc]), 0.0)
        y = y + _dot((u * u).astype(MXU_DTYPE), wdn_ref[c * fc:(c + 1) * fc, :])
    out_ref[0] = x1 + mod_ref[0, 5:6, :] * _rms(y, gpo_ref[...])


def _ffn(o_nsa, o_dsa, x, mod, g_post_mix, g_pre_ffn, g_post_ffn, w_out, w_up, w_down, tm, fc):
    B, T, D = x.shape
    dn = o_nsa.shape[-1]

    def tok(w):
        return pl.BlockSpec((1, tm, w), lambda b, i: (b, i, 0))

    def whole(a):
        return pl.BlockSpec(a.shape, lambda b, i: (0,) * a.ndim, pipeline_mode=pl.Buffered(1))

    woa, wob = w_out[:dn], w_out[dn:]
    vec = pl.BlockSpec((1, D), lambda b, i: (0, 0))
    return pl.pallas_call(
        functools.partial(_ffn_kernel, fc=fc),
        out_shape=jax.ShapeDtypeStruct((B, T, D), F32),
        grid=(B, T // tm),
        in_specs=[tok(dn), tok(o_dsa.shape[-1]), tok(D),
                  pl.BlockSpec((1, 6, D), lambda b, i: (b, 0, 0)), vec, vec, vec,
                  whole(woa), whole(wob), whole(w_up), whole(w_down)],
        out_specs=tok(D),
        compiler_params=pltpu.CompilerParams(
            dimension_semantics=("parallel", "parallel"), vmem_limit_bytes=VMEM_LIMIT),
        name="ffn",
    )(o_nsa, o_dsa, x, mod, g_post_mix, g_pre_ffn, g_post_ffn, woa, wob, w_up, w_down)


def _permute_w_in(w):
    scale = {"q_n": HEAD_DIM ** -0.5, "q_d": HEAD_DIM ** -0.5}
    cols = [w[:, _SEG[n][0]:_SEG[n][1]] * scale.get(n, 1.0) for n in _ROPE_ORDER + _PLAIN_ORDER]
    used = sum(c.shape[1] for c in cols)
    cols.append(jnp.zeros((w.shape[0], D_PROJ - used), w.dtype))
    return jnp.concatenate(cols, axis=1).astype(MXU_DTYPE)


def kernel(x, c, positions, w_ada, b_ada, g_pre_mix, g_post_mix, g_pre_ffn, g_post_ffn,
           w_in, cmp_pe_k, cmp_pe_v, cmp_w1_k, cmp_w2_k, cmp_w1_v, cmp_w2_v,
           w_out, w_up, w_down):
    B, T, D = x.shape
    depth = w_ada.shape[0]
    tm = min(TM, T)
    tk = min(TK, T)
    half = HEAD_DIM // 2
    inv = ROPE_THETA ** (-jnp.arange(half, dtype=F32) / half)
    inv_tile = jnp.tile(inv, LANES // half).reshape(1, LANES)
    pos_f = positions.astype(F32).reshape(B, T, 1)

    for l in range(depth):
        mod = _ada(c, w_ada[l], b_ada[l]).reshape(B, 6, D)
        (qn, kc, vc, ksl, vsl, kw, vw, gates, qd, kd, vd, qi, ki, wi) = _in_proj(
            x, mod, g_pre_mix[l].reshape(1, D), _permute_w_in(w_in[l]), pos_f, inv_tile, tm)
        o_nsa = _nsa(qn, kc, vc, ksl, vsl, kw, vw, gates,
                     cmp_pe_k[l].reshape(1, -1), cmp_pe_v[l].reshape(1, -1),
                     cmp_w1_k[l].astype(MXU_DTYPE), cmp_w2_k[l].astype(MXU_DTYPE),
                     cmp_w1_v[l].astype(MXU_DTYPE), cmp_w2_v[l].astype(MXU_DTYPE), tk=tk)
        o_dsa = _dsa(qd, qi, wi, kd, vd, ki, ck=tk)
        x = _ffn(o_nsa, o_dsa, x, mod, g_post_mix[l].reshape(1, D), g_pre_ffn[l].reshape(1, D),
                 g_post_ffn[l].reshape(1, D), w_out[l].astype(MXU_DTYPE),
                 w_up[l].astype(MXU_DTYPE), w_down[l].astype(MXU_DTYPE), tm=tm, fc=FFN_CHUNK)
    return x
```

```python
import functools

import numpy as np
import jax
import jax.numpy as jnp
from jax import lax
from jax.experimental import pallas as pl
from jax.experimental.pallas import tpu as pltpu

HEAD_DIM = 64
NSA_HEADS = 8
NSA_KV = 2
NSA_REP = NSA_HEADS // NSA_KV
DSA_HEADS = 8
IDX_HEADS = 4
IDX_DIM = 64
CMP_LEN = 32
CMP_STRIDE = 16
SLC_LEN = 64
SLC_TOPN = 16
WIN = 512
DSA_TOPK = 256
ROPE_THETA = 10000.0
EPS = 1e-6
NEG = -1e30
MASK_BIAS = -2e30
M_INIT = -1e30

LANES = 128
SUBLANES = 8
MXU_DTYPE = jnp.bfloat16
SOFTMAX_DTYPE = jnp.bfloat16
VMEM_LIMIT = 56 * 1024 * 1024

TQ = 2 * LANES
TQ_DSA = 4 * LANES
TM = 512
TK = 512
FFN_CHUNK = 1024
WORD = 32
F32 = jnp.float32
I32 = jnp.int32
INT_MIN = np.int32(-2 ** 31)

_SEG = dict(q_n=(0, 512), kc=(512, 640), vc=(640, 768), ksl=(768, 896), vsl=(896, 1024),
            kw=(1024, 1152), vw=(1152, 1280), gl=(1280, 1304), q_d=(1304, 1816),
            k_d=(1816, 1880), v_d=(1880, 1944), qi=(1944, 2200), ki=(2200, 2264), wi=(2264, 2268))
_ROPE_ORDER = ("q_n", "kc", "ksl", "kw", "q_d", "qi", "k_d", "ki")
_PLAIN_ORDER = ("vc", "vsl", "vw", "v_d", "gl", "wi")
D_PROJ = 18 * LANES


def _dot(a, b):
    return jnp.dot(a, b, preferred_element_type=F32)


def _dot_nt(a, b):
    return lax.dot_general(a, b, (((1,), (1,)), ((), ())), preferred_element_type=F32)


def _rms(x, g):
    return x * lax.rsqrt(jnp.mean(x * x, axis=-1, keepdims=True) + EPS) * g


def _ada_kernel(c_ref, w_ref, b_ref, o_ref):
    o_ref[...] = _dot(c_ref[...].astype(MXU_DTYPE), w_ref[...].astype(MXU_DTYPE)) + b_ref[...]


def _ada(c, w, b):
    B, D = c.shape
    n = w.shape[1] // D
    return pl.pallas_call(
        _ada_kernel,
        out_shape=jax.ShapeDtypeStruct((B, n * D), F32),
        grid=(n,),
        in_specs=[pl.BlockSpec((B, D), lambda j: (0, 0)),
                  pl.BlockSpec((D, D), lambda j: (0, j)),
                  pl.BlockSpec((1, D), lambda j: (0, j))],
        out_specs=pl.BlockSpec((B, D), lambda j: (0, j)),
        name="ada",
    )(c, w, b.reshape(1, -1))


def _inproj_kernel(x_ref, mod_ref, g_ref, w_ref, pos_ref, inv_ref,
                   qn_ref, kc_ref, vc_ref, ksl_ref, vsl_ref, kw_ref, vw_ref, gate_ref,
                   qd_ref, kd_ref, vd_ref, qi_ref, ki_ref, wi_ref, rows_ref):
    x = x_ref[0]
    tm = x.shape[0]
    h = _rms(x, g_ref[...] * (1.0 + mod_ref[0, 1:2, :])) + mod_ref[0, 0:1, :]
    proj = _dot(h.astype(MXU_DTYPE), w_ref[...])

    ang = pos_ref[0] * inv_ref[...]
    lane = lax.broadcasted_iota(I32, (tm, LANES), 1)
    first_half = (lane & (HEAD_DIM // 2)) == 0
    cosv = jnp.cos(ang)
    sinv = jnp.sin(ang)
    sin_signed = jnp.where(first_half, -sinv, sinv)

    def rope(j):
        c = proj[:, j * LANES:(j + 1) * LANES]
        swapped = jnp.where(first_half, pltpu.roll(c, LANES - HEAD_DIM // 2, 1),
                            pltpu.roll(c, HEAD_DIM // 2, 1))
        return c * cosv + swapped * sin_signed

    def plain(j):
        return proj[:, j * LANES:(j + 1) * LANES]

    dt = qn_ref.dtype
    for j in range(4):
        r = rope(j)
        qn_ref[0, 2 * j] = r[:, :HEAD_DIM].astype(dt)
        qn_ref[0, 2 * j + 1] = r[:, HEAD_DIM:].astype(dt)
    def put_rows(ref, c):
        rows_ref[...] = c
        for t in range(CMP_STRIDE):
            sub = rows_ref[pl.ds(t, tm // CMP_STRIDE, stride=CMP_STRIDE), :]
            ref[0, 0, :, t * HEAD_DIM:(t + 1) * HEAD_DIM] = sub[:, :HEAD_DIM].astype(dt)
            ref[0, 1, :, t * HEAD_DIM:(t + 1) * HEAD_DIM] = sub[:, HEAD_DIM:].astype(dt)

    put_rows(kc_ref, rope(4))
    for ref, j in ((ksl_ref, 5), (kw_ref, 6)):
        r = rope(j)
        ref[0, 0] = r[:, :HEAD_DIM].astype(dt)
        ref[0, 1] = r[:, HEAD_DIM:].astype(dt)
    for j in range(4):
        r = rope(7 + j)
        qd_ref[0, 2 * j] = r[:, :HEAD_DIM].astype(dt)
        qd_ref[0, 2 * j + 1] = r[:, HEAD_DIM:].astype(dt)
    for j in range(2):
        r = rope(11 + j)
        qi_ref[0, 2 * j] = r[:, :HEAD_DIM].astype(dt)
        qi_ref[0, 2 * j + 1] = r[:, HEAD_DIM:].astype(dt)
    r = rope(13)
    kd_ref[0] = r[:, :HEAD_DIM].astype(dt)
    ki_ref[0] = r[:, HEAD_DIM:].astype(dt)
    put_rows(vc_ref, plain(14))
    def with_ones(c):
        return jnp.where(lane < HEAD_DIM, c, 1.0).astype(dt)

    for ref, j in ((vsl_ref, 15), (vw_ref, 16)):
        v = plain(j)
        ref[0, 0] = with_ones(v)
        ref[0, 1] = with_ones(pltpu.roll(v, HEAD_DIM, 1))
    last = plain(17)
    vd_ref[0] = with_ones(last)
    ng = NSA_REP * 3
    gate_ref[0] = jax.nn.sigmoid(last)
    idx_scale = (IDX_HEADS ** -0.5) * (IDX_DIM ** -0.5)
    wi0 = HEAD_DIM + 2 * ng
    wi_ref[0] = last.T[wi0:wi0 + IDX_HEADS, :] * idx_scale


def _in_proj(x, mod, g_pre, w_perm, pos_f, inv_tile, tm):
    B, T, D = x.shape
    dt = MXU_DTYPE
    hd = HEAD_DIM

    def heads(n, w=hd):
        return (jax.ShapeDtypeStruct((B, n, T, w), dt),
                pl.BlockSpec((1, n, tm, w), lambda b, i: (b, 0, i, 0)))

    def flat(w, d=dt):
        return (jax.ShapeDtypeStruct((B, T, w), d),
                pl.BlockSpec((1, tm, w), lambda b, i: (b, i, 0)))

    row_w = CMP_STRIDE * hd
    row_view = (jax.ShapeDtypeStruct((B, NSA_KV, T // CMP_STRIDE, row_w), dt),
                pl.BlockSpec((1, NSA_KV, tm // CMP_STRIDE, row_w), lambda b, i: (b, 0, i, 0)))
    outs = [heads(NSA_HEADS),
            row_view, row_view,
            heads(NSA_KV), heads(NSA_KV, LANES),
            heads(NSA_KV), heads(NSA_KV, LANES),
            flat(LANES, F32),
            heads(DSA_HEADS),
            flat(hd), flat(LANES),
            heads(IDX_HEADS),
            flat(hd),
            (jax.ShapeDtypeStruct((B, IDX_HEADS, T), F32),
             pl.BlockSpec((1, IDX_HEADS, tm), lambda b, i: (b, 0, i)))]
    return pl.pallas_call(
        _inproj_kernel,
        out_shape=[o[0] for o in outs],
        grid=(B, T // tm),
        in_specs=[pl.BlockSpec((1, tm, D), lambda b, i: (b, i, 0)),
                  pl.BlockSpec((1, 6, D), lambda b, i: (b, 0, 0)),
                  pl.BlockSpec((1, D), lambda b, i: (0, 0)),
                  pl.BlockSpec((D, D_PROJ), lambda b, i: (0, 0)),
                  pl.BlockSpec((1, tm, 1), lambda b, i: (b, i, 0)),
                  pl.BlockSpec((1, LANES), lambda b, i: (0, 0))],
        out_specs=[o[1] for o in outs],
        scratch_shapes=[pltpu.VMEM((tm, LANES), F32)],
        compiler_params=pltpu.CompilerParams(
            dimension_semantics=("parallel", "parallel"), vmem_limit_bytes=VMEM_LIMIT),
        name="in_proj",
    )(x, mod, g_pre, w_perm, pos_f, inv_tile)


def _softmax_probs(s, bias, m_prev):
    tq, n = bias.shape
    sb = (s.astype(SOFTMAX_DTYPE).reshape(-1, tq, n) + bias.astype(SOFTMAX_DTYPE)[None]).reshape(s.shape)
    m_new = jnp.maximum(m_prev, jnp.max(sb, axis=-1, keepdims=True).astype(F32))
    return m_new, jnp.exp(sb - m_new.astype(SOFTMAX_DTYPE))


def _softmax_step(s, bias, m_prev, acc_prev, v1):
    m_new, p = _softmax_probs(s, bias, m_prev)
    acc_new = jnp.exp(m_prev - m_new) * acc_prev + _dot(p.astype(v1.dtype), v1)
    return m_new, acc_new


def _normalize(acc):
    return acc[:, :HEAD_DIM] * (1.0 / acc[:, HEAD_DIM:HEAD_DIM + 1])


def _nsa_kernel(q_ref, kc_ref, vc_ref, ksl_ref, vsl_ref, kw_ref, vw_ref, gate_ref,
                pek_ref, pev_ref, w1k_ref, w2k_ref, w1v_ref, w2v_ref, e_ref, ovlt_ref,
                o_ref, kcmp_ref, vcmp_ref, *, tk, seq):
    tq = TQ
    qi = pl.program_id(1)
    s0 = qi * tq
    ncmp = seq // CMP_STRIDE
    nslc = seq // SLC_LEN
    nsel = min(SLC_TOPN, nslc)
    rows = NSA_REP * tq
    hd = HEAD_DIM
    groups = range(NSA_KV)

    @pl.when(qi == 0)
    def _():
        half = CMP_STRIDE * hd

        def compress(r, pe_ref, w1_ref, w2_ref):
            a = _dot(r, w1_ref[:half, :])
            b = _dot(r, w1_ref[half:, :])
            pe = jnp.broadcast_to(pe_ref[...], (SUBLANES, CMP_LEN * hd)).astype(MXU_DTYPE)
            bias = _dot(pe, w1_ref[...])[0:1, :]
            pre = a + pltpu.roll(b, ncmp - 1, 0) + bias
            return _dot(jax.nn.gelu(pre).astype(MXU_DTYPE), w2_ref[...])

        for g in groups:
            kcmp_ref[g] = compress(kc_ref[0, g], pek_ref, w1k_ref, w2k_ref).astype(kcmp_ref.dtype)
            vcmp_ref[g] = compress(vc_ref[0, g], pev_ref, w1v_ref, w2v_ref).astype(vcmp_ref.dtype)

    q = [q_ref[0, g * NSA_REP:(g + 1) * NSA_REP].reshape(rows, hd) for g in groups]

    hq = tq // 2
    span = WIN + hq
    halves = range(2)
    w0 = [pl.multiple_of(jnp.maximum(s0 + hh * hq - WIN, 0), hq) for hh in halves]
    s_c = [_dot_nt(q[g], kcmp_ref[g]).reshape(NSA_REP, tq, ncmp) for g in groups]
    s_w = [[_dot_nt(q_ref[0, g * NSA_REP:(g + 1) * NSA_REP, hh * hq:(hh + 1) * hq].reshape(NSA_REP * hq, hd),
                    kw_ref[0, g, pl.ds(w0[hh], span), :]) for hh in halves] for g in groups]

    t_c = s0 + lax.broadcasted_iota(I32, (1, tq, ncmp), 1)
    n_c = lax.broadcasted_iota(I32, (1, tq, ncmp), 2)
    mask_c = (n_c * CMP_STRIDE + (CMP_LEN - 1)) <= t_c
    maskf_c = jnp.where(mask_c, 1.0, 0.0)
    o_cmp, imp = [], []
    for g in groups:
        s_m = jnp.where(mask_c, s_c[g], NEG)
        p_c = jnp.exp(s_m - jnp.max(s_m, axis=-1, keepdims=True)) * maskf_c
        l_c = jnp.sum(p_c, axis=-1, keepdims=True)
        p_c = p_c * (1.0 / jnp.maximum(l_c, 1e-30))
        o_cmp.append(_dot(p_c.reshape(rows, ncmp).astype(MXU_DTYPE), vcmp_ref[g]))
        p_sum = jnp.sum(p_c, axis=0)
        p_hi = p_sum.astype(MXU_DTYPE)
        p_lo = (p_sum - p_hi.astype(F32)).astype(MXU_DTYPE)
        imp.append(_dot_nt(ovlt_ref[...], p_hi) + _dot_nt(ovlt_ref[...], p_lo))

    bias_w = []
    for hh in halves:
        kpos = w0[hh] + lax.broadcasted_iota(I32, (hq, span), 1)
        tt = s0 + hh * hq + lax.broadcasted_iota(I32, (hq, span), 0)
        bias_w.append(jnp.where(kpos <= tt, jnp.where(kpos > tt - WIN, 0.0, MASK_BIAS), MASK_BIAS))
    o_win = []
    for g in groups:
        acc_h = []
        for hh in halves:
            _, p_w = _softmax_probs(s_w[g][hh], bias_w[hh], jnp.full((NSA_REP * hq, 1), M_INIT, F32))
            acc = _dot(p_w.astype(MXU_DTYPE), vw_ref[0, g, pl.ds(w0[hh], span), :])
            acc_h.append(acc.reshape(NSA_REP, hq, LANES))
        o_win.append(jnp.concatenate(acc_h, axis=1).reshape(rows, LANES))

    j_idx = lax.broadcasted_iota(I32, (nslc, tq), 0)
    t_s = s0 + lax.broadcasted_iota(I32, (nslc, tq), 1)
    cur = lax.shift_right_logical(t_s, int(np.log2(SLC_LEN)))
    blk_bias = []
    for g in groups:
        sc = jnp.where(j_idx == 0, jnp.inf,
                       jnp.where(j_idx == cur, jnp.inf, jnp.where(j_idx == cur - 1, jnp.inf, imp[g])))
        sc = jnp.where(j_idx <= cur, sc, -jnp.inf)
        rank = jnp.zeros((nslc, tq), I32)
        for i in range(nslc):
            ci = sc[i:i + 1, :]
            rank = rank + jnp.where(ci > sc, 1, jnp.where(ci == sc, jnp.where(j_idx > i, 1, 0), 0))
        bb = jnp.where(rank < nsel, jnp.where(j_idx <= cur, 0.0, MASK_BIAS), MASK_BIAS)
        blk_bias.append(bb.T.astype(MXU_DTYPE))

    def sel_body(kt, carry):
        off = pl.multiple_of(kt * tk, tk)
        kpos = off + lax.broadcasted_iota(I32, (tq, tk), 1)
        tt = s0 + lax.broadcasted_iota(I32, (tq, tk), 0)
        out = []
        for g in groups:
            m, acc = carry[g]
            k = ksl_ref[0, g, pl.ds(off, tk), :]
            v1 = vsl_ref[0, g, pl.ds(off, tk), :]
            bias = jnp.where(kpos <= tt, _dot(blk_bias[g], e_ref[kt]), MASK_BIAS)
            out.append(_softmax_step(_dot_nt(q[g], k), bias, m, acc, v1))
        return tuple(out)

    init = tuple((jnp.full((rows, 1), M_INIT, F32), jnp.zeros((rows, LANES), F32)) for _ in groups)
    sel = lax.fori_loop(0, s0 // tk + 1, sel_body, init)

    gt = gate_ref[0]
    outs = []
    for g in groups:
        for r in range(NSA_REP):
            sl = slice(r * tq, (r + 1) * tq)
            lane0 = HEAD_DIM + NSA_REP * 3 * g + 3 * r
            acc_s, acc_w = sel[g][1][sl], o_win[g][sl]
            w_s = gt * (1.0 / acc_s)
            w_w = gt * (1.0 / acc_w)
            outs.append(gt[:, lane0:lane0 + 1] * o_cmp[g][sl]
                        + w_s[:, lane0 + 1:lane0 + 2] * acc_s[:, :HEAD_DIM]
                        + w_w[:, lane0 + 2:lane0 + 3] * acc_w[:, :HEAD_DIM])
    o_ref[0] = jnp.concatenate(outs, axis=-1).astype(o_ref.dtype)


def _nsa(qn, kc_rows, vc_rows, ksl, vsl, kw, vw, gates, pe_k, pe_v, w1k, w2k, w1v, w2v, tk):
    B, _, T, hd = qn.shape
    tq = TQ
    ncmp = T // CMP_STRIDE
    nslc = T // SLC_LEN
    key = np.arange(T)
    e = (key[None, :] // SLC_LEN == np.arange(nslc)[:, None]).astype(np.float32)
    e = jnp.asarray(e.reshape(nslc, T // tk, tk).transpose(1, 0, 2), MXU_DTYPE)
    cmp_start = np.arange(ncmp) * CMP_STRIDE
    cmp_end = cmp_start + CMP_LEN - 1
    slc_start = np.arange(nslc) * SLC_LEN
    slc_end = slc_start + SLC_LEN - 1
    ovl = ((cmp_start[:, None] <= slc_end[None, :]) & (cmp_end[:, None] >= slc_start[None, :]))
    ovl[ncmp - 1, :] = False
    ovl_t = jnp.asarray(ovl.T.astype(np.float32), MXU_DTYPE)

    def both(shape):
        return pl.BlockSpec((1, NSA_KV) + shape, lambda b, i: (b, 0) + (0,) * len(shape))

    def whole(a):
        return pl.BlockSpec(a.shape, lambda b, i: (0,) * a.ndim)

    consts = (pe_k, pe_v, w1k, w2k, w1v, w2v, e, ovl_t)
    return pl.pallas_call(
        functools.partial(_nsa_kernel, tk=tk, seq=T),
        out_shape=jax.ShapeDtypeStruct((B, T, NSA_HEADS * hd), MXU_DTYPE),
        grid=(B, T // tq),
        in_specs=[pl.BlockSpec((1, NSA_HEADS, tq, hd), lambda b, i: (b, 0, i, 0)),
                  both((ncmp, CMP_STRIDE * hd)), both((ncmp, CMP_STRIDE * hd)),
                  both((T, hd)), both((T, LANES)), both((T, hd)), both((T, LANES)),
                  pl.BlockSpec((1, tq, LANES), lambda b, i: (b, i, 0))]
                 + [whole(a) for a in consts],
        out_specs=pl.BlockSpec((1, tq, NSA_HEADS * hd), lambda b, i: (b, i, 0)),
        scratch_shapes=[pltpu.VMEM((NSA_KV, ncmp, hd), MXU_DTYPE),
                        pltpu.VMEM((NSA_KV, ncmp, hd), MXU_DTYPE)],
        compiler_params=pltpu.CompilerParams(
            dimension_semantics=("parallel", "arbitrary"), vmem_limit_bytes=VMEM_LIMIT),
        name="nsa",
    )(qn, kc_rows, vc_rows, ksl, vsl, kw, vw, gates, *consts)


def _bit_transpose32(rows):
    a = list(rows)
    j, m = 16, 0x0000FFFF
    while j:
        for k in range(WORD):
            if k & j == 0:
                t = (a[k] ^ lax.shift_right_logical(a[k + j], np.int32(j))) & np.int32(m)
                a[k] = a[k] ^ t
                a[k + j] = a[k + j] ^ jnp.left_shift(t, np.int32(j))
        j >>= 1
        m ^= (m << j) & 0xFFFFFFFF
    return a


def _dsa_kernel(qd_ref, qi_ref, wi_ref, kd_ref, vd_ref, ki_ref, tri_ref, o_ref,
                key_ref, plane_ref, *, tq, ck, ktop):
    s0 = pl.program_id(1) * tq
    n_kc = s0 // ck + 1
    hd = HEAD_DIM
    rows = DSA_HEADS * tq
    gpc = ck // (WORD * SUBLANES)
    n_groups = plane_ref.shape[1]

    w = wi_ref[0]
    q_idx = qi_ref[0].reshape(IDX_HEADS * tq, IDX_DIM)
    krow = lax.broadcasted_iota(I32, (ck, tq), 0)
    t_q = s0 + lax.broadcasted_iota(I32, (ck, tq), 1)

    def score_body(c, carry):
        off = pl.multiple_of(c * ck, ck)
        lg = _dot_nt(ki_ref[0, pl.ds(off, ck), :], q_idx)
        sc = w[0:1, :] * jnp.maximum(lg[:, :tq], 0.0)
        for h in range(1, IDX_HEADS):
            sc = sc + w[h:h + 1, :] * jnp.maximum(lg[:, h * tq:(h + 1) * tq], 0.0)
        sc = jnp.where(sc == 0.0, 0.0, sc)
        bits = pltpu.bitcast(sc, I32)
        key = jnp.where(bits < 0, bits ^ np.int32(0x7FFFFFFF), bits)
        key = jnp.where(off + krow <= t_q, key, INT_MIN)
        key_ref[c] = key
        for g in range(gpc):
            slabs = [key[(g * WORD + j) * SUBLANES:(g * WORD + j + 1) * SUBLANES, :] ^ INT_MIN
                     for j in range(WORD)]
            for i, plane in enumerate(_bit_transpose32(slabs)):
                plane_ref[i, c * gpc + g] = plane
        return carry

    lax.fori_loop(0, n_kc, score_body, 0)

    def clear_body(c, carry):
        plane_ref[:, pl.ds(c * gpc, gpc)] = jnp.zeros((WORD, gpc, SUBLANES, tq), I32)
        return carry

    lax.fori_loop(n_kc, n_groups // gpc, clear_body, 0)

    def bit_body(i, carry):
        cand, need, thr_u = carry
        ones = cand & plane_ref[i]
        cnt = jnp.sum(jnp.sum(lax.population_count(ones), axis=0), axis=0, keepdims=True)
        take = cnt >= need
        cand = jnp.where(take, ones, cand ^ ones)
        need = jnp.where(take, need, need - cnt)
        thr_u = thr_u | jnp.where(take, jnp.left_shift(np.int32(1), np.int32(WORD - 1) - i), 0)
        return cand, need, thr_u

    cand, need, thr_u = lax.fori_loop(
        0, WORD, bit_body,
        (jnp.full((n_groups, SUBLANES, tq), -1, I32), jnp.full((1, tq), ktop, I32),
         jnp.zeros((1, tq), I32)))
    thr = thr_u ^ INT_MIN
    n_eq = jnp.sum(jnp.sum(lax.population_count(cand), axis=0), axis=0, keepdims=True)
    need_eq = need.astype(F32)
    exact_cut = jnp.max(jnp.where(thr_u == 0, 1, jnp.abs(n_eq - need))) == 0

    q = qd_ref[0].reshape(rows, hd)

    def att_body(c, carry):
        m, acc, eq_seen = carry
        off = pl.multiple_of(c * ck, ck)
        key = key_ref[c]

        def cut_bias():
            return jnp.where(key >= thr, 0.0, MASK_BIAS), eq_seen

        def tie_bias():
            eq_f = jnp.where(key == thr, 1.0, 0.0)
            before = _dot(tri_ref[...], eq_f.astype(MXU_DTYPE)) + eq_seen
            take = jnp.where(key > thr, 0.0,
                             jnp.where(key == thr, jnp.where(before < need_eq, 0.0, MASK_BIAS), MASK_BIAS))
            return (jnp.where(off + krow <= t_q, take, MASK_BIAS),
                    eq_seen + jnp.sum(eq_f, axis=0, keepdims=True))

        bias_t, eq_seen = lax.cond(exact_cut, cut_bias, tie_bias)
        bias = bias_t.T
        k = kd_ref[0, pl.ds(off, ck), :]
        v1 = vd_ref[0, pl.ds(off, ck), :]
        m, acc = _softmax_step(_dot_nt(q, k), bias, m, acc, v1)
        return m, acc, eq_seen

    init = (jnp.full((rows, 1), M_INIT, F32), jnp.zeros((rows, LANES), F32),
            jnp.zeros((1, tq), F32))
    _, acc, _ = lax.fori_loop(0, n_kc, att_body, init)
    o = _normalize(acc)
    o_ref[0] = jnp.concatenate([o[h * tq:(h + 1) * tq] for h in range(DSA_HEADS)],
                               axis=-1).astype(o_ref.dtype)


def _dsa(qd, qi, wi_t, kd, vd, ki, ck):
    B, _, T, hd = qd.shape
    tq = min(TQ_DSA, T)
    ktop = min(DSA_TOPK, T // 4)
    tri = jnp.asarray(np.tril(np.ones((ck, ck), np.float32), -1), MXU_DTYPE)
    kv_spec = pl.BlockSpec((1, T, hd), lambda b, i: (b, 0, 0))
    v_spec = pl.BlockSpec((1, T, LANES), lambda b, i: (b, 0, 0))
    return pl.pallas_call(
        functools.partial(_dsa_kernel, tq=tq, ck=ck, ktop=ktop),
        out_shape=jax.ShapeDtypeStruct((B, T, DSA_HEADS * hd), MXU_DTYPE),
        grid=(B, T // tq),
        in_specs=[pl.BlockSpec((1, DSA_HEADS, tq, hd), lambda b, i: (b, 0, i, 0)),
                  pl.BlockSpec((1, IDX_HEADS, tq, IDX_DIM), lambda b, i: (b, 0, i, 0)),
                  pl.BlockSpec((1, IDX_HEADS, tq), lambda b, i: (b, 0, i)),
                  kv_spec, v_spec, kv_spec,
                  pl.BlockSpec((ck, ck), lambda b, i: (0, 0))],
        out_specs=pl.BlockSpec((1, tq, DSA_HEADS * hd), lambda b, i: (b, i, 0)),
        scratch_shapes=[pltpu.VMEM((T // ck, ck, tq), I32),
                        pltpu.VMEM((WORD, T // (WORD * SUBLANES), SUBLANES, tq), I32)],
        compiler_params=pltpu.CompilerParams(
            dimension_semantics=("parallel", "parallel"), vmem_limit_bytes=VMEM_LIMIT),
        name="dsa",
    )(qd, qi, wi_t, kd, vd, ki, tri)


def _ffn_kernel(on_ref, od_ref, x_ref, mod_ref, gpm_ref, gpf_ref, gpo_ref,
                woa_ref, wob_ref, wup_ref, wdn_ref, out_ref, *, fc):
    o = _dot(on_ref[0], woa_ref[...]) + _dot(od_ref[0], wob_ref[...])
    x1 = x_ref[0] + mod_ref[0, 2:3, :] * _rms(o, gpm_ref[...])
    h = _rms(x1, gpf_ref[...]) * (1.0 + mod_ref[0, 4:5, :]) + mod_ref[0, 3:4, :]
    hb = h.astype(MXU_DTYPE)
    d_ff = wup_ref.shape[1]
    y = jnp.zeros(x1.shape, F32)
    for c in range(d_ff // fc):
        u = jnp.maximum(_dot(hb, wup_ref[:, c * fc:(c + 1) * fc]), 0.0)
        y = y + _dot((u * u).astype(MXU_DTYPE), wdn_ref[c * fc:(c + 1) * fc, :])
    out_ref[0] = x1 + mod_ref[0, 5:6, :] * _rms(y, gpo_ref[...])


def _ffn(o_nsa, o_dsa, x, mod, g_post_mix, g_pre_ffn, g_post_ffn, w_out, w_up, w_down, tm, fc):
    B, T, D = x.shape
    dn = o_nsa.shape[-1]

    def tok(w):
        return pl.BlockSpec((1, tm, w), lambda b, i: (b, i, 0))

    def whole(a):
        return pl.BlockSpec(a.shape, lambda b, i: (0,) * a.ndim, pipeline_mode=pl.Buffered(1))

    woa, wob = w_out[:dn], w_out[dn:]
    vec = pl.BlockSpec((1, D), lambda b, i: (0, 0))
    return pl.pallas_call(
        functools.partial(_ffn_kernel, fc=fc),
        out_shape=jax.ShapeDtypeStruct((B, T, D), F32),
        grid=(B, T // tm),
        in_specs=[tok(dn), tok(o_dsa.shape[-1]), tok(D),
                  pl.BlockSpec((1, 6, D), lambda b, i: (b, 0, 0)), vec, vec, vec,
                  whole(woa), whole(wob), whole(w_up), whole(w_down)],
        out_specs=tok(D),
        compiler_params=pltpu.CompilerParams(
            dimension_semantics=("parallel", "parallel"), vmem_limit_bytes=VMEM_LIMIT),
        name="ffn",
    )(o_nsa, o_dsa, x, mod, g_post_mix, g_pre_ffn, g_post_ffn, woa, wob, w_up, w_down)


def _permute_w_in(w):
    scale = {"q_n": HEAD_DIM ** -0.5, "q_d": HEAD_DIM ** -0.5}
    cols = [w[:, _SEG[n][0]:_SEG[n][1]] * scale.get(n, 1.0) for n in _ROPE_ORDER + _PLAIN_ORDER]
    used = sum(c.shape[1] for c in cols)
    cols.append(jnp.zeros((w.shape[0], D_PROJ - used), w.dtype))
    return jnp.concatenate(cols, axis=1).astype(MXU_DTYPE)


def kernel(x, c, positions, w_ada, b_ada, g_pre_mix, g_post_mix, g_pre_ffn, g_post_ffn,
           w_in, cmp_pe_k, cmp_pe_v, cmp_w1_k, cmp_w2_k, cmp_w1_v, cmp_w2_v,
           w_out, w_up, w_down):
    B, T, D = x.shape
    depth = w_ada.shape[0]
    tm = min(TM, T)
    tk = min(TK, T)
    half = HEAD_DIM // 2
    inv = ROPE_THETA ** (-jnp.arange(half, dtype=F32) / half)
    inv_tile = jnp.tile(inv, LANES // half).reshape(1, LANES)
    pos_f = positions.astype(F32).reshape(B, T, 1)

    for l in range(depth):
        mod = _ada(c, w_ada[l], b_ada[l]).reshape(B, 6, D)
        (qn, kc, vc, ksl, vsl, kw, vw, gates, qd, kd, vd, qi, ki, wi) = _in_proj(
            x, mod, g_pre_mix[l].reshape(1, D), _permute_w_in(w_in[l]), pos_f, inv_tile, tm)
        o_nsa = _nsa(qn, kc, vc, ksl, vsl, kw, vw, gates,
                     cmp_pe_k[l].reshape(1, -1), cmp_pe_v[l].reshape(1, -1),
                     cmp_w1_k[l].astype(MXU_DTYPE), cmp_w2_k[l].astype(MXU_DTYPE),
                     cmp_w1_v[l].astype(MXU_DTYPE), cmp_w2_v[l].astype(MXU_DTYPE), tk=tk)
        o_dsa = _dsa(qd, qi, wi, kd, vd, ki, ck=tk)
        x = _ffn(o_nsa, o_dsa, x, mod, g_post_mix[l].reshape(1, D), g_pre_ffn[l].reshape(1, D),
                 g_post_ffn[l].reshape(1, D), w_out[l].astype(MXU_DTYPE),
                 w_up[l].astype(MXU_DTYPE), w_down[l].astype(MXU_DTYPE), tm=tm, fc=FFN_CHUNK)
    return x
```
